```python
import math
import jax
import jax.numpy as jnp
from jax import lax
import numpy as np

D_MODEL = 1024
BATCH = 2
SEQ = 8192
DEPTH = 2

CTX_LEN = 256
GRID_W = 64
N_ADA = 9
D_FF = 2816
MACARON_W = 0.5
NORM_EPS = 1e-6
MIX_W = 512
CONV_K = 3

M_HEADS = 4
M_DQK = 64
M_DV = MIX_W // M_HEADS
M_QK_W = M_HEADS * M_DQK
M_CHUNK = 64

R_DH = 64
R_HEADS = MIX_W // R_DH
DECAY_LORA = 64
AAA_LORA = 64
GATE_LORA = 128
R_GN_EPS = 64e-5

A_HEADS = 4
A_DV = MIX_W // A_HEADS
A_DH = A_DV // 2
Q_BLOCK = 128
ROPE_BASE = 10000.0

IN_SPLITS = (
    ('m_q', M_QK_W), ('m_k', M_QK_W), ('m_v', MIX_W), ('m_o', MIX_W),
    ('m_if', M_HEADS), ('m_ff', M_HEADS), ('m_ib', M_HEADS), ('m_fb', M_HEADS),
    ('r_r', MIX_W), ('r_k', MIX_W), ('r_v', MIX_W),
    ('r_wf', DECAY_LORA), ('r_wb', DECAY_LORA), ('r_af', AAA_LORA), ('r_ab', AAA_LORA), ('r_g', GATE_LORA),
    ('a_q', 2 * A_HEADS * A_DH), ('a_k', 2 * A_HEADS * A_DH), ('a_v', MIX_W),
    ('g_m', D_MODEL), ('g_r', D_MODEL), ('g_a', D_MODEL),
)
D_IN = sum(w for _, w in IN_SPLITS)

kernel_name = 'hybrid_mlstm_rwkv7_diffattn_macaron_prefix'


def _rmsnorm(x, g):
    x32 = x.astype(jnp.float32)
    y = x32 * lax.rsqrt(jnp.mean(x32 * x32, axis=-1, keepdims=True) + NORM_EPS)
    return (y * g.astype(jnp.float32)).astype(x.dtype)


def _modulate(h, shift, scale):
    return h * (1 + scale) + shift


def _ada(cvec, w_ada, b_ada):
    return jnp.split(jax.nn.silu(cvec) @ w_ada + b_ada, N_ADA, axis=-1)


def _split_cols(z):
    cols, off = {}, 0
    for name, width in IN_SPLITS:
        cols[name] = z[..., off:off + width]
        off += width
    return cols


def _dwconv_centred(x, w):
    pad = CONV_K // 2
    return lax.conv_general_dilated(
        x, w[:, None, :].astype(x.dtype), window_strides=(1,), padding=[(pad, pad)],
        dimension_numbers=('NWC', 'WIO', 'NWC'), feature_group_count=x.shape[-1])


def _swiglu_half_step(z, mods, g, w_gu, w_down):
    shift, scale, gate = mods
    h = _modulate(_rmsnorm(z, g), shift, scale)
    a, b = jnp.split(h @ w_gu, 2, axis=-1)
    return z + MACARON_W * gate * ((jax.nn.silu(a) * b) @ w_down)


def _rope2d_tables(n):
    rows = n // GRID_W
    row = jnp.repeat(jnp.arange(rows, dtype=jnp.float32), GRID_W)
    col = jnp.tile(jnp.arange(GRID_W, dtype=jnp.float32), rows)
    half = A_DH // 2
    inv = ROPE_BASE ** (-jnp.arange(0, half, 2, dtype=jnp.float32) / half)
    ang = jnp.concatenate([row[:, None] * inv, col[:, None] * inv], axis=-1)
    return jnp.cos(ang), jnp.sin(ang)


def _rope2d(x, cos, sin):
    x1, x2 = x[..., 0::2], x[..., 1::2]
    c = cos[None, :, None, None, :].astype(x.dtype)
    s = sin[None, :, None, None, :].astype(x.dtype)
    return jnp.stack([x1 * c - x2 * s, x1 * s + x2 * c], axis=-1).reshape(x.shape)


def _mlstm_chunkwise(q, k, v, ig, lf, state):
    bsz, nh, n, _ = q.shape
    nc = n // M_CHUNK
    to_chunks = lambda a: jnp.moveaxis(a.reshape(bsz, nh, nc, M_CHUNK, *a.shape[3:]), 2, 0)
    tril = jnp.tril(jnp.ones((M_CHUNK, M_CHUNK), dtype=bool))

    def step(carry, inp):
        c_st, n_st, m_st = carry
        qb, kb, vb, ib, fb = inp
        bcum = jnp.cumsum(fb, axis=-1)
        dlog = jnp.where(tril, bcum[..., :, None] - bcum[..., None, :] + ib[..., None, :], -jnp.inf)
        inter = bcum + m_st[..., None]
        mt = jnp.maximum(inter, jnp.max(dlog, axis=-1))
        s = jnp.einsum('bhtd,bhsd->bhts', qb, kb) * jnp.exp(dlog - mt[..., None])
        iw = jnp.exp(inter - mt)
        num = jnp.einsum('bhts,bhsv->bhtv', s, vb) + iw[..., None] * jnp.einsum('bhvd,bhtd->bhtv', c_st, qb)
        den = jnp.sum(s, axis=-1) + iw * jnp.einsum('bhd,bhtd->bht', n_st, qb)
        h = num / jnp.maximum(jnp.abs(den), jnp.exp(-mt))[..., None]
        btot = bcum[..., -1]
        wlog = btot[..., None] - bcum + ib
        m_new = jnp.maximum(btot + m_st, jnp.max(wlog, axis=-1))
        ws = jnp.exp(wlog - m_new[..., None])
        dec = jnp.exp(btot + m_st - m_new)
        c_new = dec[..., None, None] * c_st + jnp.einsum('bhs,bhsv,bhsd->bhvd', ws, vb, kb)
        n_new = dec[..., None] * n_st + jnp.einsum('bhs,bhsd->bhd', ws, kb)
        return (c_new, n_new, m_new), h

    state, hs = lax.scan(step, state, (to_chunks(q), to_chunks(k), to_chunks(v), to_chunks(ig), to_chunks(lf)))
    return jnp.moveaxis(hs, 0, 2).reshape(bsz, nh, n, -1), state


def _mlstm_mixer(pc, pl, conv_w, gate_bias, out_norm):
    f32 = jnp.float32

    def prep(p):
        qk = jax.nn.silu(_dwconv_centred(jnp.concatenate([p['m_q'], p['m_k']], axis=-1), conv_w)).astype(f32)
        bsz, n, _ = qk.shape
        heads = lambda a, d: a.reshape(bsz, n, M_HEADS, d).transpose(0, 2, 1, 3)
        q = heads(qk[..., :M_QK_W], M_DQK)
        k = heads(qk[..., M_QK_W:], M_DQK) * (M_DQK ** -0.5)
        v = heads(p['m_v'].astype(f32), M_DV)
        gates = [p[name].astype(f32).transpose(0, 2, 1) + gate_bias[j].astype(f32)[:, None]
                 for j, name in enumerate(('m_if', 'm_ff', 'm_ib', 'm_fb'))]
        return q, k, v, gates

    qc, kc, vc, gc = prep(pc)
    ql, kl, vl, gl = prep(pl)
    bsz = ql.shape[0]
    outs = []
    for d in range(2):
        flip = (lambda a: jnp.flip(a, axis=2)) if d == 1 else (lambda a: a)
        state = (jnp.zeros((bsz, M_HEADS, M_DV, M_DQK), f32), jnp.zeros((bsz, M_HEADS, M_DQK), f32),
                 jnp.zeros((bsz, M_HEADS), f32))
        hc, state = _mlstm_chunkwise(flip(qc), flip(kc), flip(vc), flip(gc[2 * d]),
                                     flip(jax.nn.log_sigmoid(gc[2 * d + 1])), state)
        hl, _ = _mlstm_chunkwise(flip(ql), flip(kl), flip(vl), flip(gl[2 * d]),
                                 flip(jax.nn.log_sigmoid(gl[2 * d + 1])), state)
        outs.append((flip(hc), flip(hl)))

    def post(h, p):
        b, _, n, _ = h.shape
        h = _rmsnorm(h.transpose(0, 2, 1, 3), out_norm.reshape(M_HEADS, M_DV)).reshape(b, n, MIX_W)
        return (h * jax.nn.sigmoid(p['m_o'].astype(f32))).astype(p['m_o'].dtype)

    return post(outs[0][0] + outs[1][0], pc), post(outs[0][1] + outs[1][1], pl)


def _rwkv7_scan(r, w, k, v, kk, a, state, reverse):
    tm = lambda t: jnp.moveaxis(t, 1, 0)

    def step(s, inp):
        rt, wt, kt, vt, kkt, at = inp
        sa = jnp.einsum('bhvk,bhk->bhv', s, -kkt)
        s = s * wt[:, :, None, :] + sa[..., None] * (kkt * at)[:, :, None, :] + vt[..., None] * kt[:, :, None, :]
        return s, jnp.einsum('bhvk,bhk->bhv', s, rt)

    state, ys = lax.scan(step, state, (tm(r), tm(w), tm(k), tm(v), tm(kk), tm(a)), reverse=reverse)
    return jnp.moveaxis(ys, 0, 1), state


def _rwkv7_mixer(pc, pl, conv_w, w0, w2, a0, a2, g2, k_k, k_a, r_k, ln_w, ln_b):
    f32 = jnp.float32

    def prep(p):
        rkv = _dwconv_centred(jnp.concatenate([p['r_r'], p['r_k'], p['r_v']], axis=-1), conv_w).astype(f32)
        bsz, n, _ = rkv.shape
        heads = lambda t: t.reshape(bsz, n, R_HEADS, R_DH)
        r, k, v = rkv[..., :MIX_W], rkv[..., MIX_W:2 * MIX_W], rkv[..., 2 * MIX_W:]
        kk = heads(k * k_k.astype(f32))
        kk = kk / jnp.maximum(jnp.sqrt(jnp.sum(kk * kk, axis=-1, keepdims=True)), 1e-12)
        per_dir = []
        for d, (wl, al) in enumerate((('r_wf', 'r_af'), ('r_wb', 'r_ab'))):
            w_raw = w0[d].astype(f32) + jnp.tanh(p[wl].astype(f32)) @ w2[d].astype(f32)
            decay = jnp.exp(-jnp.exp(-jax.nn.softplus(-w_raw) - 0.5))
            a = jax.nn.sigmoid(a0[d].astype(f32) + p[al].astype(f32) @ a2[d].astype(f32))
            k_dir = k * (1 + (a - 1) * k_a.astype(f32))
            per_dir.append((heads(decay), heads(k_dir), heads(a)))
        return heads(r), heads(k), heads(v), kk, per_dir

    rc, kc, vc, kkc, dc = prep(pc)
    rl, kl, vl, kkl, dl = prep(pl)
    bsz = rl.shape[0]
    y_ctx, y_lat = [], []
    for d in range(2):
        s0 = jnp.zeros((bsz, R_HEADS, R_DH, R_DH), f32)
        wc, kdc, ac = dc[d]
        wlt, kdl, alt = dl[d]
        yc_d, s_ctx = _rwkv7_scan(rc, wc, kdc, vc, kkc, ac, s0, reverse=(d == 1))
        yl_d, _ = _rwkv7_scan(rl, wlt, kdl, vl, kkl, alt, s_ctx, reverse=(d == 1))
        y_ctx.append(yc_d)
        y_lat.append(yl_d)

    def post(y, r, k, v, p):
        b, n = y.shape[:2]
        mu = jnp.mean(y, axis=-1, keepdims=True)
        var = jnp.mean(jnp.square(y - mu), axis=-1, keepdims=True)
        yn = ((y - mu) * lax.rsqrt(var + R_GN_EPS)).reshape(b, n, MIX_W) * ln_w.astype(f32) + ln_b.astype(f32)
        bonus = jnp.sum(r * k * r_k.astype(f32).reshape(R_HEADS, R_DH), axis=-1, keepdims=True) * v
        g = jax.nn.sigmoid(p['r_g'].astype(f32)) @ g2.astype(f32)
        return ((yn + bonus.reshape(b, n, MIX_W)) * g).astype(p['r_g'].dtype)

    return post(y_ctx[0] + y_ctx[1], rc, kc, vc, pc), post(y_lat[0] + y_lat[1], rl, kl, vl, pl)


def _lambda_init(layer):
    return 0.8 - 0.6 * math.exp(-0.3 * layer)


def _diff_attention(pc, pl, qk_norm, lam_p, subln, lam_init, cos, sin):
    f32 = jnp.float32

    def heads(p):
        b, n, _ = p['a_q'].shape
        q = _rmsnorm(p['a_q'].reshape(b, n, A_HEADS, 2, A_DH), qk_norm[0]) * (A_DH ** -0.5)
        k = _rmsnorm(p['a_k'].reshape(b, n, A_HEADS, 2, A_DH), qk_norm[1])
        return q, k, p['a_v'].reshape(b, n, A_HEADS, A_DV)

    qc, kc, vc = heads(pc)
    ql, kl, vl = heads(pl)
    ql = _rope2d(ql, cos, sin)
    kl = _rope2d(kl, cos, sin)
    lp = lam_p.astype(f32)
    lam = jnp.exp(jnp.sum(lp[0] * lp[1])) - jnp.exp(jnp.sum(lp[2] * lp[3])) + lam_init

    def attend(q, k, v):
        s = jnp.einsum('bqhmd,bkhmd->bhmqk', q, k).astype(f32)
        pr = jax.nn.softmax(s, axis=-1)
        amap = pr[:, :, 0] - lam * pr[:, :, 1]
        return jnp.einsum('bhqk,bkhv->bqhv', amap.astype(v.dtype), v)

    yc = attend(qc, kc, vc)
    k_all = jnp.concatenate([kc, kl], axis=1)
    v_all = jnp.concatenate([vc, vl], axis=1)
    b, n = ql.shape[:2]
    nb = n // Q_BLOCK
    qb = jnp.moveaxis(ql.reshape(b, nb, Q_BLOCK, A_HEADS, 2, A_DH), 1, 0)
    yl = lax.map(lambda q: attend(q, k_all, v_all), qb)
    yl = jnp.moveaxis(yl, 0, 1).reshape(b, n, A_HEADS, A_DV)
    post = lambda y: (_rmsnorm(y, subln) * (1 - lam_init)).reshape(y.shape[0], y.shape[1], MIX_W)
    return post(yc), post(yl)


def _merge(p, ym, yr, ya, w_branch, w_o):
    z = (jax.nn.sigmoid(p['g_m']) * (ym @ w_branch[0])
         + jax.nn.sigmoid(p['g_r']) * (yr @ w_branch[1])
         + jax.nn.sigmoid(p['g_a']) * (ya @ w_branch[2]))
    return z @ w_o


def setup_inputs(seed: int = 0) -> dict:
    key = jax.random.key(seed)
    ks = jax.random.split(key, 34)
    L, D = DEPTH, D_MODEL

    def nrm(i, shape, scale):
        return jax.random.normal(ks[i], shape, jnp.float32) * scale

    def uni(i, shape):
        return jax.random.uniform(ks[i], shape, jnp.float32)

    m_gate_bias = jnp.stack([nrm(13, (L, M_HEADS), 0.1), 3.0 + 3.0 * uni(14, (L, M_HEADS)),
                             nrm(15, (L, M_HEADS), 0.1), 3.0 + 3.0 * uni(16, (L, M_HEADS))], axis=1)
    return {
        'x': nrm(0, (BATCH, SEQ, D), 1.0),
        'c': nrm(1, (BATCH, D), 1.0),
        'ctx': nrm(2, (BATCH, CTX_LEN, D), 1.0),
        'c_ctx': nrm(3, (D,), 1.0),
        'w_ada': nrm(4, (L, D, N_ADA * D), 0.5 * D ** -0.5),
        'b_ada': nrm(5, (L, N_ADA * D), 0.01),
        'norm_g': 1.0 + nrm(6, (L, 3, D), 0.02),
        'ffn1_w_gu': nrm(7, (L, D, 2 * D_FF), D ** -0.5),
        'ffn1_w_down': nrm(8, (L, D_FF, D), D_FF ** -0.5),
        'ffn2_w_gu': nrm(9, (L, D, 2 * D_FF), D ** -0.5),
        'ffn2_w_down': nrm(10, (L, D_FF, D), D_FF ** -0.5),
        'w_in': nrm(11, (L, D, D_IN), D ** -0.5),
        'm_conv': nrm(12, (L, CONV_K, 2 * M_QK_W), CONV_K ** -0.5),
        'm_gate_bias': m_gate_bias,
        'm_out_norm': 1.0 + nrm(17, (L, MIX_W), 0.02),
        'r_conv': nrm(18, (L, CONV_K, 3 * MIX_W), CONV_K ** -0.5),
        'r_w0': -6.0 + 5.0 * uni(19, (L, 2, MIX_W)),
        'r_w2': nrm(20, (L, 2, DECAY_LORA, MIX_W), 0.5 * DECAY_LORA ** -0.5),
        'r_a0': nrm(21, (L, 2, MIX_W), 0.1),
        'r_a2': nrm(22, (L, 2, AAA_LORA, MIX_W), 0.5 * AAA_LORA ** -0.5),
        'r_g2': nrm(23, (L, GATE_LORA, MIX_W), GATE_LORA ** -0.5),
        'r_kk': 0.85 + nrm(24, (L, MIX_W), 0.02),
        'r_ka': 1.0 + nrm(25, (L, MIX_W), 0.02),
        'r_rk': nrm(26, (L, MIX_W), 0.1),
        'r_ln_w': 1.0 + nrm(27, (L, MIX_W), 0.02),
        'r_ln_b': nrm(28, (L, MIX_W), 0.01),
        'a_qk_norm': 1.0 + nrm(29, (L, 2, A_DH), 0.02),
        'a_lambda': nrm(30, (L, 4, A_DH), 0.1),
        'a_subln': 1.0 + nrm(31, (L, A_DV), 0.02),
        'w_branch': nrm(32, (L, 3, MIX_W, D), MIX_W ** -0.5),
        'w_o': nrm(33, (L, D, D), D ** -0.5),
    }


def reference(x, c, ctx, c_ctx, w_ada, b_ada, norm_g, ffn1_w_gu, ffn1_w_down, ffn2_w_gu, ffn2_w_down,
              w_in, m_conv, m_gate_bias, m_out_norm, r_conv, r_w0, r_w2, r_a0, r_a2, r_g2, r_kk, r_ka,
              r_rk, r_ln_w, r_ln_b, a_qk_norm, a_lambda, a_subln, w_branch, w_o):
    cos, sin = _rope2d_tables(x.shape[1])
    xc = ctx
    for li in range(DEPTH):
        last = li == DEPTH - 1
        mod_l = [m[:, None, :] for m in _ada(c, w_ada[li], b_ada[li])]
        mod_c = _ada(c_ctx, w_ada[li], b_ada[li])
        x = _swiglu_half_step(x, mod_l[0:3], norm_g[li, 0], ffn1_w_gu[li], ffn1_w_down[li])
        xc = _swiglu_half_step(xc, mod_c[0:3], norm_g[li, 0], ffn1_w_gu[li], ffn1_w_down[li])
        hl = _modulate(_rmsnorm(x, norm_g[li, 1]), mod_l[3], mod_l[4])
        hc = _modulate(_rmsnorm(xc, norm_g[li, 1]), mod_c[3], mod_c[4])
        pl = _split_cols(hl @ w_in[li])
        pc = _split_cols(hc @ w_in[li])
        ym_c, ym_l = _mlstm_mixer(pc, pl, m_conv[li], m_gate_bias[li], m_out_norm[li])
        yr_c, yr_l = _rwkv7_mixer(pc, pl, r_conv[li], r_w0[li], r_w2[li], r_a0[li], r_a2[li], r_g2[li],
                                  r_kk[li], r_ka[li], r_rk[li], r_ln_w[li], r_ln_b[li])
        ya_c, ya_l = _diff_attention(pc, pl, a_qk_norm[li], a_lambda[li], a_subln[li], _lambda_init(li), cos, sin)
        x = x + mod_l[5] * _merge(pl, ym_l, yr_l, ya_l, w_branch[li], w_o[li])
        x = _swiglu_half_step(x, mod_l[6:9], norm_g[li, 2], ffn2_w_gu[li], ffn2_w_down[li])
        if not last:
            xc = xc + mod_c[5] * _merge(pc, ym_c, yr_c, ya_c, w_branch[li], w_o[li])
            xc = _swiglu_half_step(xc, mod_c[6:9], norm_g[li, 2], ffn2_w_gu[li], ffn2_w_down[li])
    return x
```

```python
import functools
import math

import numpy as np
import jax
import jax.numpy as jnp
from jax import lax
from jax.experimental import pallas as pl
from jax.experimental.pallas import tpu as pltpu

F32 = jnp.float32
BF16 = jnp.bfloat16
HI = lax.Precision.HIGHEST

N_ADA = 9
MACARON_W = 0.5
NORM_EPS = 1e-6
MIX_W = 512
GRID_W = 64
CHUNK = 64

M_HEADS = 4
M_DQK = 64
M_DV = 128
R_HEADS = 8
R_DH = 64
R_GN_EPS = 64e-5
A_HEADS = 4
A_DH = 64
A_DV = 128
ROPE_BASE = 10000.0

LANES = 128
SUBLANES = 8
ROW_TILE = 512
Q_TILE = 256
KV_TILE = 512

_IN_SPLITS = (
    ('m_q', 256), ('m_k', 256), ('m_v', 512), ('m_o', 512),
    ('m_if', 4), ('m_ff', 4), ('m_ib', 4), ('m_fb', 4),
    ('r_r', 512), ('r_k', 512), ('r_v', 512),
    ('r_wf', 64), ('r_wb', 64), ('r_af', 64), ('r_ab', 64), ('r_g', 128),
    ('a_q', 512), ('a_k', 512), ('a_v', 512),
    ('g_m', 1024), ('g_r', 1024), ('g_a', 1024),
)
Z_G, Z_MQK, Z_MV, Z_MO, Z_RKV, Z_AQ, Z_AK, Z_AV, Z_LORA, Z_MG, Z_W = (
    0, 3072, 3584, 4096, 4608, 6144, 6656, 7168, 7680, 8064, 8192)
Z_TILE = 1024


def _deinterleave64():
    return np.concatenate([np.arange(0, 64, 2), np.arange(1, 64, 2)])


def _z_column_sources():
    off, start = {}, 0
    for name, w in _IN_SPLITS:
        off[name] = start
        start += w
    cols = []
    rng = lambda n: list(range(off[n], off[n] + dict(_IN_SPLITS)[n]))
    cols += rng('g_m') + rng('g_r') + rng('g_a')
    cols += rng('m_q') + rng('m_k') + rng('m_v') + rng('m_o')
    cols += rng('r_r') + rng('r_k') + rng('r_v')
    perm = _deinterleave64()
    for n in ('a_q', 'a_k'):
        for g in range(8):
            cols += list(off[n] + g * 64 + perm)
    cols += rng('a_v')
    cols += rng('r_wf') + rng('r_wb') + rng('r_af') + rng('r_ab') + rng('r_g')
    cols += rng('m_if') + rng('m_ff') + rng('m_ib') + rng('m_fb')
    cols += [-1] * (Z_W - len(cols))
    assert len(cols) == Z_W
    return np.asarray(cols, np.int32)


def _sigmoid(x):
    return 1.0 / (1.0 + jnp.exp(-x))


def _dot(a, b, prec=None):
    return jnp.dot(a, b, preferred_element_type=F32, precision=prec)


def _dot_nt(a, b, prec=None):
    return lax.dot_general(a, b, (((1,), (1,)), ((), ())), preferred_element_type=F32, precision=prec)


def _dot_tn(a, b, prec=None):
    return lax.dot_general(a, b, (((0,), (0,)), ((), ())), preferred_element_type=F32, precision=prec)


def _norm_mod(x, g, shift, scale):
    ms = jnp.mean(x * x, axis=-1, keepdims=True)
    y = x * lax.rsqrt(ms + NORM_EPS) * g
    return y * (1.0 + scale) + shift


def _ada_kernel(c_ref, w_ref, b_ref, o_ref):
    c = c_ref[...]
    s = c * _sigmoid(c)
    o_ref[0] = _dot(s, w_ref[0], HI) + b_ref[0]


def _ada_all(cv, w_ada, b_ada):
    depth, d, nd = w_ada.shape
    tn = 1024
    return pl.pallas_call(
        _ada_kernel,
        grid=(depth, nd // tn),
        in_specs=[pl.BlockSpec((SUBLANES, d), lambda l, n: (0, 0)),
                  pl.BlockSpec((1, d, tn), lambda l, n: (l, 0, n)),
                  pl.BlockSpec((1, 1, tn), lambda l, n: (l, 0, n))],
        out_specs=pl.BlockSpec((1, SUBLANES, tn), lambda l, n: (l, 0, n)),
        out_shape=jax.ShapeDtypeStruct((depth, SUBLANES, nd), F32),
        name='ada',
    )(cv, w_ada, b_ada.reshape(depth, 1, nd))


def _ffn_kernel(x_ref, mod_ref, g_ref, wa_ref, wb_ref, wd_ref, o_ref, h_scr, acc_scr, *, mi):
    f = pl.program_id(1)

    @pl.when(f == 0)
    def _():
        h = _norm_mod(x_ref[...], g_ref[...], mod_ref[0, mi:mi + 1, :], mod_ref[0, mi + 1:mi + 2, :])
        h_scr[...] = h.astype(BF16)
        acc_scr[...] = jnp.zeros_like(acc_scr)

    h = h_scr[...]
    a = _dot(h, wa_ref[...])
    b = _dot(h, wb_ref[...])
    u = (a * _sigmoid(a)) * b
    acc_scr[...] += _dot(u.astype(BF16), wd_ref[...])

    @pl.when(f == pl.num_programs(1) - 1)
    def _():
        o_ref[...] = x_ref[...] + MACARON_W * mod_ref[0, mi + 2:mi + 3, :] * acc_scr[...]


def _ffn(xr, mods, g, w_gu, w_down, mi, rows, stream_of_tile):
    d = xr.shape[1]
    dff = w_down.shape[0]
    tf = 256
    nf = dff // tf
    tm = ROW_TILE
    return pl.pallas_call(
        functools.partial(_ffn_kernel, mi=mi),
        grid=(rows // tm, nf),
        in_specs=[pl.BlockSpec((tm, d), lambda i, f: (i, 0)),
                  pl.BlockSpec((1, N_ADA, d), lambda i, f: (stream_of_tile(i), 0, 0)),
                  pl.BlockSpec((1, d), lambda i, f: (0, 0)),
                  pl.BlockSpec((d, tf), lambda i, f: (0, f)),
                  pl.BlockSpec((d, tf), lambda i, f: (0, nf + f)),
                  pl.BlockSpec((tf, d), lambda i, f: (f, 0))],
        out_specs=pl.BlockSpec((tm, d), lambda i, f: (i, 0)),
        out_shape=jax.ShapeDtypeStruct((rows, d), F32),
        scratch_shapes=[pltpu.VMEM((tm, d), BF16), pltpu.VMEM((tm, d), F32)],
        compiler_params=pltpu.CompilerParams(dimension_semantics=("parallel", "arbitrary")),
        name='ffn',
    )(xr, mods, g.reshape(1, d), w_gu, w_gu, w_down)


def _inproj_kernel(x_ref, mod_ref, g_ref, w_ref, o_ref, h_scr):
    @pl.when(pl.program_id(1) == 0)
    def _():
        h = _norm_mod(x_ref[...], g_ref[...], mod_ref[0, 3:4, :], mod_ref[0, 4:5, :])
        h_scr[...] = h.astype(BF16)

    o_ref[...] = _dot(h_scr[...], w_ref[...])


def _inproj(xr, mods, g, w_z, stream_of_tile):
    rows, d = xr.shape
    tm = ROW_TILE
    return pl.pallas_call(
        _inproj_kernel,
        grid=(rows // tm, Z_W // Z_TILE),
        in_specs=[pl.BlockSpec((tm, d), lambda i, n: (i, 0)),
                  pl.BlockSpec((1, N_ADA, d), lambda i, n: (stream_of_tile(i), 0, 0)),
                  pl.BlockSpec((1, d), lambda i, n: (0, 0)),
                  pl.BlockSpec((d, Z_TILE), lambda i, n: (0, n))],
        out_specs=pl.BlockSpec((tm, Z_TILE), lambda i, n: (i, n)),
        out_shape=jax.ShapeDtypeStruct((rows, Z_W), F32),
        scratch_shapes=[pltpu.VMEM((tm, d), BF16)],
        compiler_params=pltpu.CompilerParams(dimension_semantics=("parallel", "arbitrary")),
        name='inproj',
    )(xr, mods, g.reshape(1, d), w_z)


def _conv3(x, prev8, next8, w, row0, geom):
    nl, seq, ctx = geom
    tm = x.shape[0]
    t = lax.broadcasted_iota(jnp.int32, (tm, 1), 0)
    r = row0 + t
    is_lat = r < nl
    pos = jnp.where(is_lat, lax.rem(r, seq), lax.rem(jnp.maximum(r - nl, 0), ctx))
    seglen = jnp.where(is_lat, seq, ctx)
    xm = pltpu.roll(x, 1, 0)
    xm = jnp.where(t == 0, prev8[SUBLANES - 1:SUBLANES, :], xm)
    xm = jnp.where(pos == 0, 0.0, xm)
    xp = pltpu.roll(x, tm - 1, 0)
    xp = jnp.where(t == tm - 1, next8[0:1, :], xp)
    xp = jnp.where(pos == seglen - 1, 0.0, xp)
    return xm * w[0:1, :] + x * w[1:2, :] + xp * w[2:3, :]


def _halo_specs(tm, width, col_blk, rows):
    per = tm // SUBLANES
    last = rows // SUBLANES - 1
    return [pl.BlockSpec((tm, width), lambda i: (i, col_blk)),
            pl.BlockSpec((SUBLANES, width), lambda i: (jnp.maximum(i * per - 1, 0), col_blk)),
            pl.BlockSpec((SUBLANES, width), lambda i: (jnp.minimum((i + 1) * per, last), col_blk))]


def _mprep_kernel(x_ref, xp_ref, xn_ref, w_ref, gt_ref, gb_ref, qk_ref, go_ref, *, geom):
    tm = x_ref.shape[0]
    y = _conv3(x_ref[...], xp_ref[...], xn_ref[...], w_ref[...], pl.program_id(0) * tm, geom)
    y = y * _sigmoid(y)
    lane = lax.broadcasted_iota(jnp.int32, (1, y.shape[1]), 1)
    qk_ref[...] = jnp.where(lane >= M_HEADS * M_DQK, y * (M_DQK ** -0.5), y)
    g = gt_ref[...] + gb_ref[...]
    gl = lax.broadcasted_iota(jnp.int32, (1, LANES), 1)
    is_forget = (lax.rem(gl, 2 * M_HEADS) >= M_HEADS) & (gl < 4 * M_HEADS)
    logsig = jnp.minimum(g, 0.0) - jnp.log(1.0 + jnp.exp(-jnp.abs(g)))
    go_ref[...] = jnp.where(is_forget, logsig, g)


def _mprep(z, conv_w, gate_bias, geom):
    rows = z.shape[0]
    tm = ROW_TILE
    gb = jnp.zeros((1, LANES), F32).at[0, :4 * M_HEADS].set(gate_bias.reshape(-1))
    return pl.pallas_call(
        functools.partial(_mprep_kernel, geom=geom),
        grid=(rows // tm,),
        in_specs=_halo_specs(tm, 512, Z_MQK // 512, rows) + [
            pl.BlockSpec((3, 512), lambda i: (0, 0)),
            pl.BlockSpec((tm, LANES), lambda i: (i, Z_MG // LANES)),
            pl.BlockSpec((1, LANES), lambda i: (0, 0))],
        out_specs=[pl.BlockSpec((tm, 512), lambda i: (i, 0)),
                   pl.BlockSpec((tm, LANES), lambda i: (i, 0))],
        out_shape=[jax.ShapeDtypeStruct((rows, 512), F32), jax.ShapeDtypeStruct((rows, LANES), F32)],
        compiler_params=pltpu.CompilerParams(dimension_semantics=("parallel",)),
        name='mlstm_prep',
    )(z, z, z, conv_w, z, gb)


def _chunk_block(d, b, j, geom):
    nl, seq, ctx = geom
    nc_c, nc_l = ctx // CHUNK, seq // CHUNK
    jl = j - nc_c
    c_ctx = jnp.where(d == 0, j, nc_c - 1 - j)
    c_lat = jnp.where(d == 0, jl, nc_l - 1 - jl)
    return jnp.where(j < nc_c, nl // CHUNK + b * nc_c + c_ctx, b * nc_l + c_lat)


def _order_masks(d):
    ti = lax.broadcasted_iota(jnp.int32, (CHUNK, CHUNK), 0)
    si = lax.broadcasted_iota(jnp.int32, (CHUNK, CHUNK), 1)
    fwd = d == 0
    ahead = jnp.where(fwd, si - ti, ti - si)
    incl = ahead <= 0
    strict = ahead < 0
    incl_t = ahead >= 0
    return fwd, incl, strict, incl_t, ti == si


def _mlstm_kernel(qk_ref, v_ref, gc_ref, gr_ref, o_ref, ct_scr, n_scr, m_scr):
    d = pl.program_id(0)

    @pl.when(pl.program_id(2) == 0)
    def _():
        ct_scr[...] = jnp.zeros_like(ct_scr)
        n_scr[...] = jnp.zeros_like(n_scr)
        m_scr[...] = jnp.zeros_like(m_scr)

    fwd, incl, _, incl_t, _ = _order_masks(d)
    gc = gc_ref[...]
    gr = gr_ref[0]
    for h in range(M_HEADS):
        i_col = jnp.where(fwd, gc[:, h:h + 1], gc[:, 8 + h:9 + h])
        f_col = jnp.where(fwd, gc[:, 4 + h:5 + h], gc[:, 12 + h:13 + h])
        i_row = jnp.where(fwd, gr[h:h + 1, :], gr[8 + h:9 + h, :])
        f_row = jnp.where(fwd, gr[4 + h:5 + h, :], gr[12 + h:13 + h, :])
        bcum_col = jnp.sum(jnp.where(incl, f_row, 0.0), axis=1, keepdims=True)
        bcum_row = jnp.sum(jnp.where(incl_t, f_col, 0.0), axis=0, keepdims=True)
        dlog = jnp.where(incl, bcum_col - bcum_row + i_row, -jnp.inf)
        m_st = m_scr[h]
        inter = bcum_col + m_st
        mt = jnp.maximum(inter, jnp.max(dlog, axis=1, keepdims=True))
        q = qk_ref[h]
        k = qk_ref[M_HEADS + h]
        v = v_ref[:, h * M_DV:(h + 1) * M_DV]
        s = _dot_nt(q, k, HI) * jnp.exp(dlog - mt)
        iw = jnp.exp(inter - mt)
        ct = ct_scr[h]
        nvec = n_scr[h]
        num = _dot(s, v, HI) + iw * _dot(q, ct, HI)
        den = jnp.sum(s, axis=1, keepdims=True) + iw * jnp.sum(q * nvec, axis=1, keepdims=True)
        o_ref[0, :, h * M_DV:(h + 1) * M_DV] = num / jnp.maximum(jnp.abs(den), jnp.exp(-mt))
        btot = jnp.sum(f_col, axis=0, keepdims=True)
        wlog = btot - bcum_col + i_col
        m_new = jnp.maximum(btot + m_st, jnp.max(wlog, axis=0, keepdims=True))
        ws = jnp.exp(wlog - m_new)
        dec = jnp.exp(btot + m_st - m_new)
        ct_scr[h] = dec * ct + _dot_tn(k, ws * v, HI)
        n_scr[h] = dec * nvec + jnp.sum(ws * k, axis=0, keepdims=True)
        m_scr[h] = m_new


def _mlstm_scan(qk_hm, z, gcol, grow, geom, batch):
    rows = z.shape[0]
    nch = (geom[1] + geom[2]) // CHUNK
    blk = lambda d, b, j: _chunk_block(d, b, j, geom)
    return pl.pallas_call(
        _mlstm_kernel,
        grid=(2, batch, nch),
        in_specs=[pl.BlockSpec((2 * M_HEADS, CHUNK, M_DQK), lambda d, b, j: (0, blk(d, b, j), 0)),
                  pl.BlockSpec((CHUNK, MIX_W), lambda d, b, j: (blk(d, b, j), Z_MV // MIX_W)),
                  pl.BlockSpec((CHUNK, LANES), lambda d, b, j: (blk(d, b, j), 0)),
                  pl.BlockSpec((1, 16, CHUNK), lambda d, b, j: (blk(d, b, j), 0, 0))],
        out_specs=pl.BlockSpec((1, CHUNK, MIX_W), lambda d, b, j: (d, blk(d, b, j), 0)),
        out_shape=jax.ShapeDtypeStruct((2, rows, MIX_W), F32),
        scratch_shapes=[pltpu.VMEM((M_HEADS, M_DQK, M_DV), F32),
                        pltpu.VMEM((M_HEADS, 1, M_DQK), F32),
                        pltpu.VMEM((M_HEADS, 1, 1), F32)],
        compiler_params=pltpu.CompilerParams(dimension_semantics=("arbitrary", "arbitrary", "arbitrary")),
        name='mlstm_scan',
    )(qk_hm, z, gcol, grow)


def _rprep_kernel(x_ref, xp_ref, xn_ref, cw_ref, lo_ref, w0_ref, w2_ref, a0_ref, a2_ref, g2_ref,
                  rkv_ref, lw_ref, aa_ref, g_ref, *, geom):
    tm = x_ref.shape[0]
    rkv_ref[...] = _conv3(x_ref[...], xp_ref[...], xn_ref[...], cw_ref[...], pl.program_id(0) * tm, geom)
    lo = lo_ref[...]
    for d in range(2):
        w_raw = w0_ref[d:d + 1, :] + _dot(jnp.tanh(lo[:, 64 * d:64 * d + 64]), w2_ref[d], HI)
        lw_ref[d] = -_sigmoid(w_raw) * math.exp(-0.5)
        aa_ref[d] = _sigmoid(a0_ref[d:d + 1, :] + _dot(lo[:, 128 + 64 * d:192 + 64 * d], a2_ref[d], HI))
    g_ref[...] = _dot(_sigmoid(lo[:, 256:384]), g2_ref[...], HI)


def _rprep(z, conv_w, w0, w2, a0, a2, g2, geom):
    rows = z.shape[0]
    tm = ROW_TILE
    full = lambda *s: pl.BlockSpec(s, lambda i: (0,) * len(s))
    return pl.pallas_call(
        functools.partial(_rprep_kernel, geom=geom),
        grid=(rows // tm,),
        in_specs=_halo_specs(tm, 1536, Z_RKV // 1536, rows) + [
            full(3, 1536),
            pl.BlockSpec((tm, 384), lambda i: (i, Z_LORA // 384)),
            full(2, 512), full(2, 64, 512), full(2, 512), full(2, 64, 512), full(128, 512)],
        out_specs=[pl.BlockSpec((tm, 1536), lambda i: (i, 0)),
                   pl.BlockSpec((2, tm, 512), lambda i: (0, i, 0)),
                   pl.BlockSpec((2, tm, 512), lambda i: (0, i, 0)),
                   pl.BlockSpec((tm, 512), lambda i: (i, 0))],
        out_shape=[jax.ShapeDtypeStruct((rows, 1536), F32), jax.ShapeDtypeStruct((2, rows, 512), F32),
                   jax.ShapeDtypeStruct((2, rows, 512), F32), jax.ShapeDtypeStruct((rows, 512), F32)],
        compiler_params=pltpu.CompilerParams(dimension_semantics=("parallel",)),
        name='rwkv_prep',
    )(z, z, z, conv_w, z, w0, w2, a0, a2, g2)


def _rwkv_kernel(rkv_ref, lw_ref, aa_ref, kkw_ref, kaw_ref, y_ref, s_scr):
    d = pl.program_id(0)

    @pl.when(pl.program_id(2) == 0)
    def _():
        s_scr[...] = jnp.zeros_like(s_scr)

    _, incl, strict, _, eye = _order_masks(d)
    ones_incl = incl.astype(F32)
    eye_f = eye.astype(F32)
    for h in range(R_HEADS):
        r = rkv_ref[0, h]
        k = rkv_ref[1, h]
        v = rkv_ref[2, h]
        lw = lw_ref[0, h]
        a = aa_ref[0, h]
        kkr = k * kkw_ref[h]
        kk = kkr / jnp.maximum(jnp.sqrt(jnp.sum(kkr * kkr, axis=1, keepdims=True)), 1e-12)
        kd = k * (1.0 + (a - 1.0) * kaw_ref[h])
        cum = _dot(ones_incl, lw, HI)
        tot = jnp.sum(lw, axis=0, keepdims=True)
        e_neg = jnp.exp(-cum)
        e_end = jnp.exp(tot - cum)
        kka = kk * a
        at = -kk * jnp.exp(cum - lw)
        bt = kka * e_neg
        kt = kd * e_neg
        rt = r * jnp.exp(cum)
        bb = kka * e_end
        kb = kd * e_end
        aab = jnp.where(strict, _dot_nt(at, bt, HI), 0.0)
        aak = jnp.where(strict, _dot_nt(at, kt, HI), 0.0)
        arb = jnp.where(incl, _dot_nt(rt, bt, HI), 0.0)
        ark = jnp.where(incl, _dot_nt(rt, kt, HI), 0.0)
        tinv = eye_f + aab
        pw = aab
        for _ in range(int(math.log2(CHUNK)) - 1):
            pw = _dot(pw, pw, HI)
            tinv = tinv + _dot(tinv, pw, HI)
        ah = _dot(tinv, at, HI)
        w0 = _dot(tinv, _dot(aak, v, HI), HI)
        rh = rt + _dot(arb, ah, HI)
        y0 = _dot(arb, w0, HI) + _dot(ark, v, HI)
        mm = jnp.where(eye, jnp.exp(tot), 0.0) + _dot_tn(ah, bb, HI)
        nn = _dot_tn(w0, bb, HI) + _dot_tn(v, kb, HI)
        s = s_scr[h]
        y_ref[0, h] = _dot_nt(rh, s, HI) + y0
        s_scr[h] = _dot(s, mm, HI) + nn


def _rwkv_scan(rkv_hm, lw_hm, aa_hm, kk_w, ka_w, geom, batch):
    rows = rkv_hm.shape[2]
    nch = (geom[1] + geom[2]) // CHUNK
    blk = lambda d, b, j: _chunk_block(d, b, j, geom)
    return pl.pallas_call(
        _rwkv_kernel,
        grid=(2, batch, nch),
        in_specs=[pl.BlockSpec((3, R_HEADS, CHUNK, R_DH), lambda d, b, j: (0, 0, blk(d, b, j), 0)),
                  pl.BlockSpec((1, R_HEADS, CHUNK, R_DH), lambda d, b, j: (d, 0, blk(d, b, j), 0)),
                  pl.BlockSpec((1, R_HEADS, CHUNK, R_DH), lambda d, b, j: (d, 0, blk(d, b, j), 0)),
                  pl.BlockSpec((R_HEADS, 1, R_DH), lambda d, b, j: (0, 0, 0)),
                  pl.BlockSpec((R_HEADS, 1, R_DH), lambda d, b, j: (0, 0, 0))],
        out_specs=pl.BlockSpec((1, R_HEADS, CHUNK, R_DH), lambda d, b, j: (d, 0, blk(d, b, j), 0)),
        out_shape=jax.ShapeDtypeStruct((2, R_HEADS, rows, R_DH), F32),
        scratch_shapes=[pltpu.VMEM((R_HEADS, R_DH, R_DH), F32)],
        compiler_params=pltpu.CompilerParams(dimension_semantics=("arbitrary", "arbitrary", "arbitrary")),
        name='rwkv_scan',
    )(rkv_hm, lw_hm, aa_hm, kk_w.reshape(R_HEADS, 1, R_DH), ka_w.reshape(R_HEADS, 1, R_DH))


def _rpost_kernel(y_ref, rkv_ref, g_ref, lnw_ref, lnb_ref, rk_ref, bd_ref, o_ref):
    y = y_ref[0] + y_ref[1]
    bd = bd_ref[...]
    inv = 1.0 / R_DH
    mu = _dot(y, bd, HI) * inv
    yc = y - mu
    var = _dot(yc * yc, bd, HI) * inv
    yn = yc * lax.rsqrt(var + R_GN_EPS) * lnw_ref[...] + lnb_ref[...]
    r = rkv_ref[:, 0:512]
    k = rkv_ref[:, 512:1024]
    v = rkv_ref[:, 1024:1536]
    bonus = _dot(r * k * rk_ref[...], bd, HI) * v
    o_ref[...] = (yn + bonus) * g_ref[...]


def _block_diag_ones(n, group):
    i = np.arange(n) // group
    return jnp.asarray((i[:, None] == i[None, :]).astype(np.float32))


def _rpost(y_tm, rkv_c, g, ln_w, ln_b, r_k):
    rows = rkv_c.shape[0]
    tm = ROW_TILE
    full = lambda *s: pl.BlockSpec(s, lambda i: (0,) * len(s))
    return pl.pallas_call(
        _rpost_kernel,
        grid=(rows // tm,),
        in_specs=[pl.BlockSpec((2, tm, 512), lambda i: (0, i, 0)),
                  pl.BlockSpec((tm, 1536), lambda i: (i, 0)),
                  pl.BlockSpec((tm, 512), lambda i: (i, 0)),
                  full(1, 512), full(1, 512), full(1, 512), full(512, 512)],
        out_specs=pl.BlockSpec((tm, 512), lambda i: (i, 0)),
        out_shape=jax.ShapeDtypeStruct((rows, 512), F32),
        compiler_params=pltpu.CompilerParams(dimension_semantics=("parallel",)),
        name='rwkv_post',
    )(y_tm, rkv_c, g, ln_w.reshape(1, 512), ln_b.reshape(1, 512), r_k.reshape(1, 512),
      _block_diag_ones(512, R_DH))


def _aprep_kernel(q_ref, k_ref, cos_ref, sin_ref, qn_ref, kn_ref, bd_ref, qo_ref, ko_ref):
    bd = bd_ref[...]
    cos = cos_ref[...]
    sin = sin_ref[...]
    lane = lax.broadcasted_iota(jnp.int32, (1, 512), 1)
    lower = lax.rem(lane, A_DH) < A_DH // 2

    def one(x, g):
        ms = _dot(x * x, bd, HI) * (1.0 / A_DH)
        y = x * lax.rsqrt(ms + NORM_EPS) * g
        swapped = jnp.where(lower, pltpu.roll(y, 512 - A_DH // 2, 1), pltpu.roll(y, A_DH // 2, 1))
        return y * cos + swapped * sin

    qo_ref[...] = one(q_ref[...], qn_ref[...]).astype(BF16)
    ko_ref[...] = one(k_ref[...], kn_ref[...]).astype(BF16)


def _rope_tables(seq, tm):
    rows = seq // GRID_W
    row = jnp.repeat(jnp.arange(rows, dtype=F32), GRID_W)
    col = jnp.tile(jnp.arange(GRID_W, dtype=F32), rows)
    half = A_DH // 2
    inv = ROPE_BASE ** (-jnp.arange(0, half, 2, dtype=F32) / half)
    ang = jnp.concatenate([row[:, None] * inv, col[:, None] * inv], axis=-1)
    cos, sin = jnp.cos(ang), jnp.sin(ang)
    cos64 = jnp.concatenate([cos, cos], axis=-1)
    sin64 = jnp.concatenate([-sin, sin], axis=-1)
    cos_t = jnp.concatenate([jnp.tile(cos64, (1, 8)), jnp.ones((tm, 512), F32)], axis=0)
    sin_t = jnp.concatenate([jnp.tile(sin64, (1, 8)), jnp.zeros((tm, 512), F32)], axis=0)
    return cos_t, sin_t


def _aprep(z, cos_t, sin_t, qk_norm, geom):
    rows = z.shape[0]
    nl, seq, _ = geom
    tm = ROW_TILE
    perm = _deinterleave64()
    qn = jnp.tile(qk_norm[0][perm] * (A_DH ** -0.5), 8).reshape(1, 512)
    kn = jnp.tile(qk_norm[1][perm], 8).reshape(1, 512)
    tiles_per_seq = seq // tm
    tab = lambda i: (jnp.where(i < nl // tm, lax.rem(i, tiles_per_seq), tiles_per_seq), 0)
    full = lambda *s: pl.BlockSpec(s, lambda i: (0,) * len(s))
    return pl.pallas_call(
        _aprep_kernel,
        grid=(rows // tm,),
        in_specs=[pl.BlockSpec((tm, 512), lambda i: (i, Z_AQ // 512)),
                  pl.BlockSpec((tm, 512), lambda i: (i, Z_AK // 512)),
                  pl.BlockSpec((tm, 512), tab), pl.BlockSpec((tm, 512), tab),
                  full(1, 512), full(1, 512), full(512, 512)],
        out_specs=[pl.BlockSpec((tm, 512), lambda i: (i, 0)), pl.BlockSpec((tm, 512), lambda i: (i, 0))],
        out_shape=[jax.ShapeDtypeStruct((rows, 512), BF16), jax.ShapeDtypeStruct((rows, 512), BF16)],
        compiler_params=pltpu.CompilerParams(dimension_semantics=("parallel",)),
        name='attn_prep',
    )(z, z, cos_t, sin_t, qn, kn, _block_diag_ones(512, A_DH))


def _attn_kernel(*refs, n_lat, lam_init, aliased):
    if n_lat:
        lam_ref, sub_ref, q_ref, kc_ref, vc_ref, kl_ref, vl_ref = refs[:7]
        rest = refs[7:]
    else:
        lam_ref, sub_ref, q_ref, kc_ref, vc_ref = refs[:5]
        rest = refs[5:]
    if aliased:
        rest = rest[1:]
    o_ref, m_scr, l_scr, acc_scr = rest

    q = q_ref[...]
    lane = lax.broadcasted_iota(jnp.int32, (1, LANES), 1)
    zero = jnp.zeros_like(q)
    qs = (jnp.where(lane < A_DH, q, zero), jnp.where(lane >= A_DH, q, zero))
    m_scr[...] = jnp.full_like(m_scr, -jnp.inf)
    l_scr[...] = jnp.zeros_like(l_scr)
    acc_scr[...] = jnp.zeros_like(acc_scr)

    def step(kb, vb):
        vb = vb.astype(BF16)
        for mi in range(2):
            s = _dot_nt(qs[mi], kb)
            m_old = m_scr[mi]
            m_new = jnp.maximum(m_old, jnp.max(s, axis=1, keepdims=True))
            alpha = jnp.exp(m_old - m_new)
            p = jnp.exp(s - m_new)
            l_scr[mi] = alpha * l_scr[mi] + jnp.sum(p, axis=1, keepdims=True)
            acc_scr[mi] = alpha * acc_scr[mi] + _dot(p.astype(BF16), vb)
            m_scr[mi] = m_new

    step(kc_ref[...], vc_ref[...])
    if n_lat:
        def body(c, carry):
            off = pl.multiple_of(c * KV_TILE, KV_TILE)
            step(kl_ref[pl.ds(off, KV_TILE), :], vl_ref[pl.ds(off, KV_TILE), :])
            return carry
        lax.fori_loop(0, n_lat, body, 0)

    lp = lam_ref[...]
    lam = (jnp.exp(jnp.sum(lp[0:1, :] * lp[1:2, :], axis=1, keepdims=True))
           - jnp.exp(jnp.sum(lp[2:3, :] * lp[3:4, :], axis=1, keepdims=True)) + lam_init)
    o = acc_scr[0] / l_scr[0] - lam * (acc_scr[1] / l_scr[1])
    ms = jnp.mean(o * o, axis=-1, keepdims=True)
    o_ref[...] = o * lax.rsqrt(ms + NORM_EPS) * sub_ref[...] * (1.0 - lam_init)


def _attention(qr, kr, z, lam_p, subln, lam_init, geom, batch, ya=None):
    rows = z.shape[0]
    nl, seq, ctx = geom
    vcol = Z_AV // LANES
    ctx_blk = nl // ctx
    full = lambda *s: pl.BlockSpec(s, lambda b, h, i: (0,) * len(s))
    kv_ctx = [pl.BlockSpec((ctx, LANES), lambda b, h, i: (ctx_blk + b, h)),
              pl.BlockSpec((ctx, LANES), lambda b, h, i: (ctx_blk + b, vcol + h))]
    if ya is None:
        tq = Q_TILE
        nq = seq // tq
        qmap = lambda b, h, i: (b * nq + i, h)
        in_specs = [full(4, A_DH), full(1, A_DV), pl.BlockSpec((tq, LANES), qmap)] + kv_ctx + [
            pl.BlockSpec((seq, LANES), lambda b, h, i: (b, h)),
            pl.BlockSpec((seq, LANES), lambda b, h, i: (b, vcol + h))]
        args = (lam_p, subln.reshape(1, A_DV), qr, kr, z, kr, z)
        n_lat, aliases = seq // KV_TILE, {}
    else:
        tq, nq = ctx, 1
        qmap = lambda b, h, i: (ctx_blk + b, h)
        in_specs = [full(4, A_DH), full(1, A_DV), pl.BlockSpec((tq, LANES), qmap)] + kv_ctx + [
            pl.BlockSpec(memory_space=pl.ANY)]
        args = (lam_p, subln.reshape(1, A_DV), qr, kr, z, ya)
        n_lat, aliases = 0, {5: 0}
    return pl.pallas_call(
        functools.partial(_attn_kernel, n_lat=n_lat, lam_init=lam_init, aliased=ya is not None),
        grid=(batch, A_HEADS, nq),
        in_specs=in_specs,
        out_specs=pl.BlockSpec((tq, LANES), qmap),
        out_shape=jax.ShapeDtypeStruct((rows, MIX_W), F32),
        scratch_shapes=[pltpu.VMEM((2, tq, 1), F32), pltpu.VMEM((2, tq, 1), F32),
                        pltpu.VMEM((2, tq, A_DV), F32)],
        input_output_aliases=aliases,
        compiler_params=pltpu.CompilerParams(dimension_semantics=("parallel", "parallel", "parallel")),
        name='diff_attn' if ya is None else 'diff_attn_ctx',
    )(*args)


def _merge_kernel(x_ref, mod_ref, hm_ref, mo_ref, on_ref, yr_ref, ya_ref, g_ref, wb_ref, wo_ref, o_ref):
    hm = hm_ref[0] + hm_ref[1]
    on = on_ref[...]
    parts = []
    for h in range(M_HEADS):
        blk = hm[:, h * M_DV:(h + 1) * M_DV]
        ms = jnp.mean(blk * blk, axis=-1, keepdims=True)
        parts.append(blk * lax.rsqrt(ms + NORM_EPS) * on[:, h * M_DV:(h + 1) * M_DV])
    ym = jnp.concatenate(parts, axis=1) * _sigmoid(mo_ref[...])
    d = x_ref.shape[1]
    zsum = _sigmoid(g_ref[:, 0:d]) * _dot(ym.astype(BF16), wb_ref[0])
    zsum += _sigmoid(g_ref[:, d:2 * d]) * _dot(yr_ref[...].astype(BF16), wb_ref[1])
    zsum += _sigmoid(g_ref[:, 2 * d:3 * d]) * _dot(ya_ref[...].astype(BF16), wb_ref[2])
    o_ref[...] = x_ref[...] + mod_ref[0, 5:6, :] * _dot(zsum.astype(BF16), wo_ref[...])


def _merge(xr, mods, hm, z, out_norm, yr, ya, w_branch, w_o, rows, stream_of_tile):
    d = xr.shape[1]
    tm = ROW_TILE
    full = lambda *s: pl.BlockSpec(s, lambda i: (0,) * len(s))
    return pl.pallas_call(
        _merge_kernel,
        grid=(rows // tm,),
        in_specs=[pl.BlockSpec((tm, d), lambda i: (i, 0)),
                  pl.BlockSpec((1, N_ADA, d), lambda i: (stream_of_tile(i), 0, 0)),
                  pl.BlockSpec((2, tm, MIX_W), lambda i: (0, i, 0)),
                  pl.BlockSpec((tm, MIX_W), lambda i: (i, Z_MO // MIX_W)),
                  full(1, MIX_W),
                  pl.BlockSpec((tm, MIX_W), lambda i: (i, 0)),
                  pl.BlockSpec((tm, MIX_W), lambda i: (i, 0)),
                  pl.BlockSpec((tm, 3 * d), lambda i: (i, Z_G // (3 * d))),
                  full(3, MIX_W, d), full(d, d)],
        out_specs=pl.BlockSpec((tm, d), lambda i: (i, 0)),
        out_shape=jax.ShapeDtypeStruct((rows, d), F32),
        compiler_params=pltpu.CompilerParams(dimension_semantics=("parallel",)),
        name='merge',
    )(xr, mods, hm, z, out_norm.reshape(1, MIX_W), yr, ya, z, w_branch, w_o)


def _lambda_init(layer):
    return 0.8 - 0.6 * math.exp(-0.3 * layer)


def kernel(x, c, ctx, c_ctx, w_ada, b_ada, norm_g, ffn1_w_gu, ffn1_w_down, ffn2_w_gu, ffn2_w_down,
           w_in, m_conv, m_gate_bias, m_out_norm, r_conv, r_w0, r_w2, r_a0, r_a2, r_g2, r_kk, r_ka,
           r_rk, r_ln_w, r_ln_b, a_qk_norm, a_lambda, a_subln, w_branch, w_o):
    batch, seq, d = x.shape
    ctx_len = ctx.shape[1]
    depth = w_ada.shape[0]
    nl, nc = batch * seq, batch * ctx_len
    rows = nl + nc
    geom = (nl, seq, ctx_len)
    tm = ROW_TILE
    assert seq % tm == 0 and nc % tm == 0 and seq % KV_TILE == 0 and nl % ctx_len == 0
    assert ctx_len % CHUNK == 0 and seq % GRID_W == 0 and d == 1024

    tiles_per_seq = seq // tm
    stream_of_tile = lambda i: jnp.minimum(i // tiles_per_seq, batch)

    cv = jnp.zeros((SUBLANES, d), F32).at[:batch].set(c).at[batch].set(c_ctx)
    mods_all = _ada_all(cv, w_ada, b_ada)[:, :batch + 1].reshape(depth, batch + 1, N_ADA, d)

    zcols = _z_column_sources()
    zsrc = jnp.asarray(np.maximum(zcols, 0))
    zmask = jnp.asarray((zcols >= 0).astype(np.float32))
    cos_t, sin_t = _rope_tables(seq, tm)

    xr = jnp.concatenate([x.reshape(nl, d), ctx.reshape(nc, d)], axis=0)
    for li in range(depth):
        last = li == depth - 1
        mods = mods_all[li]
        w_z = (jnp.take(w_in[li], zsrc, axis=1) * zmask).astype(BF16)

        xr = _ffn(xr, mods, norm_g[li, 0], ffn1_w_gu[li].astype(BF16), ffn1_w_down[li].astype(BF16),
                  0, rows, stream_of_tile)
        z = _inproj(xr, mods, norm_g[li, 1], w_z, stream_of_tile)

        qk_act, gcol = _mprep(z, m_conv[li], m_gate_bias[li], geom)
        qk_hm = qk_act.reshape(rows, 2 * M_HEADS, M_DQK).transpose(1, 0, 2)
        grow = gcol[:, :16].reshape(rows // CHUNK, CHUNK, 16).transpose(0, 2, 1)
        hm = _mlstm_scan(qk_hm, z, gcol, grow, geom, batch)

        rkv_c, lw, aa, g_r = _rprep(z, r_conv[li], r_w0[li], r_w2[li], r_a0[li], r_a2[li], r_g2[li], geom)
        rkv_hm = rkv_c.reshape(rows, 3, R_HEADS, R_DH).transpose(1, 2, 0, 3)
        to_hm = lambda t: t.reshape(2, rows, R_HEADS, R_DH).transpose(0, 2, 1, 3)
        y_hm = _rwkv_scan(rkv_hm, to_hm(lw), to_hm(aa), r_kk[li], r_ka[li], geom, batch)
        y_tm = y_hm.transpose(0, 2, 1, 3).reshape(2, rows, MIX_W)
        yr = _rpost(y_tm, rkv_c, g_r, r_ln_w[li], r_ln_b[li], r_rk[li])

        qr, kr = _aprep(z, cos_t, sin_t, a_qk_norm[li], geom)
        ya = _attention(qr, kr, z, a_lambda[li], a_subln[li], _lambda_init(li), geom, batch)
        if not last:
            ya = _attention(qr, kr, z, a_lambda[li], a_subln[li], _lambda_init(li), geom, batch, ya=ya)

        out_rows = nl if last else rows
        xr = _merge(xr, mods, hm, z, m_out_norm[li], yr, ya, w_branch[li].astype(BF16),
                    w_o[li].astype(BF16), out_rows, stream_of_tile)
        xr = _ffn(xr, mods, norm_g[li, 2], ffn2_w_gu[li].astype(BF16), ffn2_w_down[li].astype(BF16),
                  6, out_rows, stream_of_tile)
    return xr[:nl].reshape(batch, seq, d)
```

```python
import functools
import math

import numpy as np
import jax
import jax.numpy as jnp
from jax import lax
from jax.experimental import pallas as pl
from jax.experimental.pallas import tpu as pltpu

F32 = jnp.float32
BF16 = jnp.bfloat16
HI = lax.Precision.HIGHEST

N_ADA = 9
MACARON_W = 0.5
NORM_EPS = 1e-6
MIX_W = 512
GRID_W = 64
CHUNK = 64

M_HEADS = 4
M_DQK = 64
M_DV = 128
R_HEADS = 8
R_DH = 64
R_GN_EPS = 64e-5
A_HEADS = 4
A_DH = 64
A_DV = 128
ROPE_BASE = 10000.0

LANES = 128
SUBLANES = 8
ROW_TILE = 512
RWKV_PREP_TILE = 256
Q_TILE = 512
KV_TILE = 512
V_PAD = 16

_IN_SPLITS = (
    ('m_q', 256), ('m_k', 256), ('m_v', 512), ('m_o', 512),
    ('m_if', 4), ('m_ff', 4), ('m_ib', 4), ('m_fb', 4),
    ('r_r', 512), ('r_k', 512), ('r_v', 512),
    ('r_wf', 64), ('r_wb', 64), ('r_af', 64), ('r_ab', 64), ('r_g', 128),
    ('a_q', 512), ('a_k', 512), ('a_v', 512),
    ('g_m', 1024), ('g_r', 1024), ('g_a', 1024),
)
Z_G, Z_MQK, Z_MV, Z_MO, Z_RKV, Z_AQ, Z_AK, Z_AV, Z_LORA, Z_MG, Z_W = (
    0, 3072, 3584, 4096, 4608, 6144, 6656, 7168, 7680, 8064, 8192)
Z_TILE = 1024


def _deinterleave64():
    return np.concatenate([np.arange(0, 64, 2), np.arange(1, 64, 2)])


def _z_column_sources():
    off, start = {}, 0
    for name, w in _IN_SPLITS:
        off[name] = start
        start += w
    cols = []
    rng = lambda n: list(range(off[n], off[n] + dict(_IN_SPLITS)[n]))
    cols += rng('g_m') + rng('g_r') + rng('g_a')
    cols += rng('m_q') + rng('m_k') + rng('m_v') + rng('m_o')
    cols += rng('r_r') + rng('r_k') + rng('r_v')
    perm = _deinterleave64()
    for n in ('a_q', 'a_k'):
        for g in range(8):
            cols += list(off[n] + g * 64 + perm)
    cols += rng('a_v')
    cols += rng('r_wf') + rng('r_wb') + rng('r_af') + rng('r_ab') + rng('r_g')
    cols += rng('m_if') + rng('m_ff') + rng('m_ib') + rng('m_fb')
    cols += [-1] * (Z_W - len(cols))
    assert len(cols) == Z_W
    return np.asarray(cols, np.int32)


def _sigmoid(x):
    return 1.0 / (1.0 + jnp.exp(-x))


def _dot(a, b, prec=None):
    return jnp.dot(a, b, preferred_element_type=F32, precision=prec)


def _dot_nt(a, b, prec=None):
    return lax.dot_general(a, b, (((1,), (1,)), ((), ())), preferred_element_type=F32, precision=prec)


def _dot_tn(a, b, prec=None):
    return lax.dot_general(a, b, (((0,), (0,)), ((), ())), preferred_element_type=F32, precision=prec)


def _norm_mod(x, g, shift, scale):
    ms = jnp.mean(x * x, axis=-1, keepdims=True)
    y = x * lax.rsqrt(ms + NORM_EPS) * g
    return y * (1.0 + scale) + shift


def _ada_kernel(c_ref, w_ref, b_ref, o_ref):
    c = c_ref[...]
    s = c * _sigmoid(c)
    o_ref[0] = _dot(s, w_ref[0], HI) + b_ref[0]


def _ada_all(cv, w_ada, b_ada):
    depth, d, nd = w_ada.shape
    tn = 1024
    return pl.pallas_call(
        _ada_kernel,
        grid=(depth, nd // tn),
        in_specs=[pl.BlockSpec((SUBLANES, d), lambda l, n: (0, 0)),
                  pl.BlockSpec((1, d, tn), lambda l, n: (l, 0, n)),
                  pl.BlockSpec((1, 1, tn), lambda l, n: (l, 0, n))],
        out_specs=pl.BlockSpec((1, SUBLANES, tn), lambda l, n: (l, 0, n)),
        out_shape=jax.ShapeDtypeStruct((depth, SUBLANES, nd), F32),
        name='ada',
    )(cv, w_ada, b_ada.reshape(depth, 1, nd))


def _ffn_kernel(x_ref, mod_ref, g_ref, wa_ref, wb_ref, wd_ref, o_ref, h_scr, acc_scr, *, mi):
    f = pl.program_id(1)

    @pl.when(f == 0)
    def _():
        h = _norm_mod(x_ref[...], g_ref[...], mod_ref[0, mi:mi + 1, :], mod_ref[0, mi + 1:mi + 2, :])
        h_scr[...] = h.astype(BF16)
        acc_scr[...] = jnp.zeros_like(acc_scr)

    h = h_scr[...]
    a = _dot(h, wa_ref[...])
    b = _dot(h, wb_ref[...])
    u = (a * _sigmoid(a)) * b
    acc_scr[...] += _dot(u.astype(BF16), wd_ref[...])

    @pl.when(f == pl.num_programs(1) - 1)
    def _():
        o_ref[...] = x_ref[...] + MACARON_W * mod_ref[0, mi + 2:mi + 3, :] * acc_scr[...]


def _ffn(xr, mods, g, w_gu, w_down, mi, rows, stream_of_tile):
    d = xr.shape[1]
    dff = w_down.shape[0]
    tf = 256
    nf = dff // tf
    tm = ROW_TILE
    return pl.pallas_call(
        functools.partial(_ffn_kernel, mi=mi),
        grid=(rows // tm, nf),
        in_specs=[pl.BlockSpec((tm, d), lambda i, f: (i, 0)),
                  pl.BlockSpec((1, N_ADA, d), lambda i, f: (stream_of_tile(i), 0, 0)),
                  pl.BlockSpec((1, d), lambda i, f: (0, 0)),
                  pl.BlockSpec((d, tf), lambda i, f: (0, f)),
                  pl.BlockSpec((d, tf), lambda i, f: (0, nf + f)),
                  pl.BlockSpec((tf, d), lambda i, f: (f, 0))],
        out_specs=pl.BlockSpec((tm, d), lambda i, f: (i, 0)),
        out_shape=jax.ShapeDtypeStruct((rows, d), F32),
        scratch_shapes=[pltpu.VMEM((tm, d), BF16), pltpu.VMEM((tm, d), F32)],
        compiler_params=pltpu.CompilerParams(dimension_semantics=("arbitrary", "arbitrary")),
        name='ffn',
    )(xr, mods, g.reshape(1, d), w_gu, w_gu, w_down)


def _inproj_kernel(x_ref, mod_ref, g_ref, w_ref, o_ref, h_scr):
    @pl.when(pl.program_id(1) == 0)
    def _():
        h = _norm_mod(x_ref[...], g_ref[...], mod_ref[0, 3:4, :], mod_ref[0, 4:5, :])
        h_scr[...] = h.astype(BF16)

    o_ref[...] = _dot(h_scr[...], w_ref[...])


def _inproj(xr, mods, g, w_z, stream_of_tile):
    rows, d = xr.shape
    tm = ROW_TILE
    return pl.pallas_call(
        _inproj_kernel,
        grid=(rows // tm, Z_W // Z_TILE),
        in_specs=[pl.BlockSpec((tm, d), lambda i, n: (i, 0)),
                  pl.BlockSpec((1, N_ADA, d), lambda i, n: (stream_of_tile(i), 0, 0)),
                  pl.BlockSpec((1, d), lambda i, n: (0, 0)),
                  pl.BlockSpec((d, Z_TILE), lambda i, n: (0, n))],
        out_specs=pl.BlockSpec((tm, Z_TILE), lambda i, n: (i, n)),
        out_shape=jax.ShapeDtypeStruct((rows, Z_W), F32),
        scratch_shapes=[pltpu.VMEM((tm, d), BF16)],
        compiler_params=pltpu.CompilerParams(dimension_semantics=("arbitrary", "arbitrary")),
        name='inproj',
    )(xr, mods, g.reshape(1, d), w_z)


def _conv3(x, prev8, next8, w, row0, geom):
    nl, seq, ctx = geom
    tm = x.shape[0]
    t = lax.broadcasted_iota(jnp.int32, (tm, 1), 0)
    r = row0 + t
    is_lat = r < nl
    pos = jnp.where(is_lat, lax.rem(r, seq), lax.rem(jnp.maximum(r - nl, 0), ctx))
    seglen = jnp.where(is_lat, seq, ctx)
    xm = pltpu.roll(x, 1, 0)
    xm = jnp.where(t == 0, prev8[SUBLANES - 1:SUBLANES, :], xm)
    xm = jnp.where(pos == 0, 0.0, xm)
    xp = pltpu.roll(x, tm - 1, 0)
    xp = jnp.where(t == tm - 1, next8[0:1, :], xp)
    xp = jnp.where(pos == seglen - 1, 0.0, xp)
    return xm * w[0:1, :] + x * w[1:2, :] + xp * w[2:3, :]


def _halo_specs(tm, width, col_blk, rows):
    per = tm // SUBLANES
    last = rows // SUBLANES - 1
    return [pl.BlockSpec((tm, width), lambda i: (i, col_blk)),
            pl.BlockSpec((SUBLANES, width), lambda i: (jnp.maximum(i * per - 1, 0), col_blk)),
            pl.BlockSpec((SUBLANES, width), lambda i: (jnp.minimum((i + 1) * per, last), col_blk))]


def _mprep_kernel(x_ref, xp_ref, xn_ref, w_ref, gt_ref, gb_ref, qk_ref, go_ref, *, geom):
    tm = x_ref.shape[0]
    y = _conv3(x_ref[...], xp_ref[...], xn_ref[...], w_ref[...], pl.program_id(0) * tm, geom)
    y = y * _sigmoid(y)
    lane = lax.broadcasted_iota(jnp.int32, (1, y.shape[1]), 1)
    qk_ref[...] = jnp.where(lane >= M_HEADS * M_DQK, y * (M_DQK ** -0.5), y)
    g = gt_ref[...] + gb_ref[...]
    gl = lax.broadcasted_iota(jnp.int32, (1, LANES), 1)
    is_forget = (lax.rem(gl, 2 * M_HEADS) >= M_HEADS) & (gl < 4 * M_HEADS)
    logsig = jnp.minimum(g, 0.0) - jnp.log(1.0 + jnp.exp(-jnp.abs(g)))
    go_ref[...] = jnp.where(is_forget, logsig, g)


def _mprep(z, conv_w, gate_bias, geom):
    rows = z.shape[0]
    tm = ROW_TILE
    gb = jnp.zeros((1, LANES), F32).at[0, :4 * M_HEADS].set(gate_bias.reshape(-1))
    return pl.pallas_call(
        functools.partial(_mprep_kernel, geom=geom),
        grid=(rows // tm,),
        in_specs=_halo_specs(tm, 512, Z_MQK // 512, rows) + [
            pl.BlockSpec((3, 512), lambda i: (0, 0)),
            pl.BlockSpec((tm, LANES), lambda i: (i, Z_MG // LANES)),
            pl.BlockSpec((1, LANES), lambda i: (0, 0))],
        out_specs=[pl.BlockSpec((tm, 512), lambda i: (i, 0)),
                   pl.BlockSpec((tm, LANES), lambda i: (i, 0))],
        out_shape=[jax.ShapeDtypeStruct((rows, 512), F32), jax.ShapeDtypeStruct((rows, LANES), F32)],
        compiler_params=pltpu.CompilerParams(dimension_semantics=("arbitrary",)),
        name='mlstm_prep',
    )(z, z, z, conv_w, z, gb)


def _chunk_block(d, b, j, geom):
    nl, seq, ctx = geom
    nc_c, nc_l = ctx // CHUNK, seq // CHUNK
    jl = j - nc_c
    c_ctx = jnp.where(d == 0, j, nc_c - 1 - j)
    c_lat = jnp.where(d == 0, jl, nc_l - 1 - jl)
    return jnp.where(j < nc_c, nl // CHUNK + b * nc_c + c_ctx, b * nc_l + c_lat)


def _order_masks(d):
    ti = lax.broadcasted_iota(jnp.int32, (CHUNK, CHUNK), 0)
    si = lax.broadcasted_iota(jnp.int32, (CHUNK, CHUNK), 1)
    fwd = d == 0
    ahead = jnp.where(fwd, si - ti, ti - si)
    incl = ahead <= 0
    strict = ahead < 0
    incl_t = ahead >= 0
    return fwd, incl, strict, incl_t, ti == si


def _mlstm_kernel(qk_ref, v_ref, gc_ref, gr_ref, o_ref, ct_scr, n_scr, m_scr):
    d = pl.program_id(0)

    @pl.when(pl.program_id(2) == 0)
    def _():
        ct_scr[...] = jnp.zeros_like(ct_scr)
        n_scr[...] = jnp.zeros_like(n_scr)
        m_scr[...] = jnp.zeros_like(m_scr)

    fwd, incl, _, incl_t, _ = _order_masks(d)
    gc = gc_ref[...]
    gr = gr_ref[0]
    heads = range(M_HEADS)
    q = [qk_ref[h] for h in heads]
    k = [qk_ref[M_HEADS + h] for h in heads]
    qb = [a.astype(BF16) for a in q]
    kb = [a.astype(BF16) for a in k]
    v = [v_ref[:, h * M_DV:(h + 1) * M_DV] for h in heads]
    ct = [ct_scr[h] for h in heads]
    qk = [_dot_nt(qb[h], kb[h]) for h in heads]
    qc = [_dot(qb[h], ct[h].astype(BF16)) for h in heads]
    s, iw, mt, ws, dec, m_new = [], [], [], [], [], []
    for h in heads:
        i_col = jnp.where(fwd, gc[:, h:h + 1], gc[:, 8 + h:9 + h])
        f_col = jnp.where(fwd, gc[:, 4 + h:5 + h], gc[:, 12 + h:13 + h])
        i_row = jnp.where(fwd, gr[h:h + 1, :], gr[8 + h:9 + h, :])
        f_row = jnp.where(fwd, gr[4 + h:5 + h, :], gr[12 + h:13 + h, :])
        bcum_col = jnp.sum(jnp.where(incl, f_row, 0.0), axis=1, keepdims=True)
        bcum_row = jnp.sum(jnp.where(incl_t, f_col, 0.0), axis=0, keepdims=True)
        dlog = jnp.where(incl, bcum_col - bcum_row + i_row, -jnp.inf)
        m_st = m_scr[h]
        inter = bcum_col + m_st
        mt.append(jnp.maximum(inter, jnp.max(dlog, axis=1, keepdims=True)))
        s.append(qk[h] * jnp.exp(dlog - mt[h]))
        iw.append(jnp.exp(inter - mt[h]))
        btot = jnp.sum(f_col, axis=0, keepdims=True)
        wlog = btot - bcum_col + i_col
        m_new.append(jnp.maximum(btot + m_st, jnp.max(wlog, axis=0, keepdims=True)))
        ws.append(jnp.exp(wlog - m_new[h]))
        dec.append(jnp.exp(btot + m_st - m_new[h]))
    sv = [_dot(s[h].astype(BF16), v[h].astype(BF16)) for h in heads]
    kv = [_dot_tn(kb[h], (ws[h] * v[h]).astype(BF16)) for h in heads]
    for h in heads:
        nvec = n_scr[h]
        num = sv[h] + iw[h] * qc[h]
        den = jnp.sum(s[h], axis=1, keepdims=True) + iw[h] * jnp.sum(q[h] * nvec, axis=1, keepdims=True)
        o_ref[0, :, h * M_DV:(h + 1) * M_DV] = num / jnp.maximum(jnp.abs(den), jnp.exp(-mt[h]))
        ct_scr[h] = dec[h] * ct[h] + kv[h]
        n_scr[h] = dec[h] * nvec + jnp.sum(ws[h] * k[h], axis=0, keepdims=True)
        m_scr[h] = m_new[h]


def _mlstm_scan(qk_hm, z, gcol, grow, geom, batch):
    rows = z.shape[0]
    nch = (geom[1] + geom[2]) // CHUNK
    blk = lambda d, b, j: _chunk_block(d, b, j, geom)
    return pl.pallas_call(
        _mlstm_kernel,
        grid=(2, batch, nch),
        in_specs=[pl.BlockSpec((2 * M_HEADS, CHUNK, M_DQK), lambda d, b, j: (0, blk(d, b, j), 0)),
                  pl.BlockSpec((CHUNK, MIX_W), lambda d, b, j: (blk(d, b, j), Z_MV // MIX_W)),
                  pl.BlockSpec((CHUNK, LANES), lambda d, b, j: (blk(d, b, j), 0)),
                  pl.BlockSpec((1, 16, CHUNK), lambda d, b, j: (blk(d, b, j), 0, 0))],
        out_specs=pl.BlockSpec((1, CHUNK, MIX_W), lambda d, b, j: (d, blk(d, b, j), 0)),
        out_shape=jax.ShapeDtypeStruct((2, rows, MIX_W), F32),
        scratch_shapes=[pltpu.VMEM((M_HEADS, M_DQK, M_DV), F32),
                        pltpu.VMEM((M_HEADS, 1, M_DQK), F32),
                        pltpu.VMEM((M_HEADS, 1, 1), F32)],
        compiler_params=pltpu.CompilerParams(dimension_semantics=("arbitrary", "arbitrary", "arbitrary")),
        name='mlstm_scan',
    )(qk_hm, z, gcol, grow)


def _rprep_kernel(x_ref, xp_ref, xn_ref, cw_ref, lo_ref, w0_ref, w2_ref, a0_ref, a2_ref, g2_ref,
                  kkw_ref, kaw_ref, bd_ref, tri_ref,
                  rkv_ref, g_ref, vb_ref, at_ref, bt_ref, kt_ref, rt_ref, bb_ref, kb_ref, et_ref, *, geom):
    tm = x_ref.shape[0]
    rkv = _conv3(x_ref[...], xp_ref[...], xn_ref[...], cw_ref[...], pl.program_id(0) * tm, geom)
    rkv_ref[...] = rkv
    r = rkv[:, 0:MIX_W]
    k = rkv[:, MIX_W:2 * MIX_W]
    vb_ref[...] = rkv[:, 2 * MIX_W:3 * MIX_W].astype(BF16)
    lo = lo_ref[...]
    g_ref[...] = _dot(_sigmoid(lo[:, 256:384]), g2_ref[...], HI)
    kkr = k * kkw_ref[...]
    kk = kkr / jnp.maximum(jnp.sqrt(_dot(kkr * kkr, bd_ref[...], HI)), 1e-12)
    for d in range(2):
        w_raw = w0_ref[d:d + 1, :] + _dot(jnp.tanh(lo[:, 64 * d:64 * d + 64]), w2_ref[d], HI)
        lw = -_sigmoid(w_raw) * math.exp(-0.5)
        a = _sigmoid(a0_ref[d:d + 1, :] + _dot(lo[:, 128 + 64 * d:192 + 64 * d], a2_ref[d], HI))
        hi = lw.astype(BF16)
        rem = lw - hi.astype(F32)
        mid = rem.astype(BF16)
        low = (rem - mid.astype(F32)).astype(BF16)
        parts = jnp.concatenate([hi, mid, low], axis=1)
        c3 = _dot(tri_ref[d, 0], parts)
        s3 = _dot(tri_ref[d, 1], parts)
        cum = c3[:, 0:MIX_W] + c3[:, MIX_W:2 * MIX_W] + c3[:, 2 * MIX_W:3 * MIX_W]
        suf = s3[:, 0:MIX_W] + s3[:, MIX_W:2 * MIX_W] + s3[:, 2 * MIX_W:3 * MIX_W]
        kd = k * (1.0 + (a - 1.0) * kaw_ref[...])
        kka = kk * a
        e_neg = jnp.exp(-cum)
        e_end = jnp.exp(suf)
        at_ref[d] = (-kk * jnp.exp(cum - lw)).astype(BF16)
        bt_ref[d] = (kka * e_neg).astype(BF16)
        kt_ref[d] = (kd * e_neg).astype(BF16)
        rt_ref[d] = (r * jnp.exp(cum)).astype(BF16)
        bb_ref[d] = (kka * e_end).astype(BF16)
        kb_ref[d] = (kd * e_end).astype(BF16)
        et_ref[d] = jnp.exp(cum + suf)


def _chunk_order_matrices(tm):
    i = np.arange(tm)
    same = (i[:, None] // CHUNK) == (i[None, :] // CHUNK)
    le = i[None, :] <= i[:, None]
    ge = i[None, :] >= i[:, None]
    mats = np.stack([np.stack([same & le, same & ~le]), np.stack([same & ge, same & ~ge])])
    return jnp.asarray(mats.astype(np.float32), dtype=BF16)


def _rprep(z, conv_w, w0, w2, a0, a2, g2, kk_w, ka_w, geom):
    rows = z.shape[0]
    tm = RWKV_PREP_TILE
    full = lambda *s: pl.BlockSpec(s, lambda i: (0,) * len(s))
    row = lambda w: pl.BlockSpec((tm, w), lambda i: (i, 0))
    both = pl.BlockSpec((2, tm, MIX_W), lambda i: (0, i, 0))
    shp = lambda dt: jax.ShapeDtypeStruct((2, rows, MIX_W), dt)
    return pl.pallas_call(
        functools.partial(_rprep_kernel, geom=geom),
        grid=(rows // tm,),
        in_specs=_halo_specs(tm, 1536, Z_RKV // 1536, rows) + [
            full(3, 1536),
            pl.BlockSpec((tm, 384), lambda i: (i, Z_LORA // 384)),
            full(2, 512), full(2, 64, 512), full(2, 512), full(2, 64, 512), full(128, 512),
            full(1, MIX_W), full(1, MIX_W), full(MIX_W, MIX_W), full(2, 2, tm, tm)],
        out_specs=[row(1536), row(MIX_W), row(MIX_W)] + [both] * 7,
        out_shape=[jax.ShapeDtypeStruct((rows, 1536), F32), jax.ShapeDtypeStruct((rows, MIX_W), F32),
                   jax.ShapeDtypeStruct((rows, MIX_W), BF16)] + [shp(BF16)] * 6 + [shp(F32)],
        compiler_params=pltpu.CompilerParams(dimension_semantics=("arbitrary",)),
        name='rwkv_prep',
    )(z, z, z, conv_w, z, w0, w2, a0, a2, g2, kk_w.reshape(1, MIX_W), ka_w.reshape(1, MIX_W),
      _block_diag_ones(MIX_W, R_DH), _chunk_order_matrices(tm))


def _rwkv_kernel(at_ref, bt_ref, kt_ref, rt_ref, bb_ref, kb_ref, v_ref, et_ref, y_ref, ss_scr):
    d = pl.program_id(0)

    @pl.when(pl.program_id(2) == 0)
    def _():
        ss_scr[...] = jnp.zeros_like(ss_scr)

    ti = lax.broadcasted_iota(jnp.int32, (CHUNK, LANES), 0)
    li = lax.broadcasted_iota(jnp.int32, (CHUNK, LANES), 1)
    si = jnp.bitwise_and(li, R_DH - 1)
    ahead = jnp.where(d == 0, si - ti, ti - si)
    strict2 = ahead < 0
    incl2 = ahead <= 0
    first = li < R_DH
    eye64 = (lax.broadcasted_iota(jnp.int32, (CHUNK, R_DH), 0)
             == lax.broadcasted_iota(jnp.int32, (CHUNK, R_DH), 1)).astype(F32)
    rr = lax.broadcasted_iota(jnp.int32, (LANES, LANES), 0)
    cc = lax.broadcasted_iota(jnp.int32, (LANES, LANES), 1)
    same_head = (rr < R_DH) == (cc < R_DH)
    eye128 = rr == cc
    zero = jnp.zeros((CHUNK, LANES), BF16)
    cat0 = lambda xs: jnp.concatenate(xs, axis=0)
    cat1 = lambda xs: jnp.concatenate(xs, axis=1)
    keep = lambda h, x: jnp.where(first, x, zero) if h == 0 else jnp.where(first, zero, x)

    pairs = range(R_HEADS // 2)
    sls = [slice(p * LANES, (p + 1) * LANES) for p in pairs]
    at = [at_ref[0, :, sl] for sl in sls]
    rt = [rt_ref[0, :, sl] for sl in sls]
    v = [v_ref[:, sl] for sl in sls]
    g = [_dot_nt(cat0([keep(0, at[p]), keep(0, rt[p]), keep(1, at[p]), keep(1, rt[p])]),
                 cat0([bt_ref[0, :, sls[p]], kt_ref[0, :, sls[p]]])) for p in pairs]
    ga = [[jnp.where(strict2, g[p][2 * h * CHUNK:(2 * h + 1) * CHUNK], 0.0) for h in range(2)] for p in pairs]
    gr = [[jnp.where(incl2, g[p][(2 * h + 1) * CHUNK:(2 * h + 2) * CHUNK], 0.0) for h in range(2)] for p in pairs]
    aakv = [_dot(cat1(ga[p]).astype(BF16), cat0([zero, keep(0, v[p]), zero, keep(1, v[p])])).astype(BF16)
            for p in pairs]
    pw = [ga[p][h][:, :R_DH] for p in pairs for h in range(2)]
    tinv = [eye64 + a for a in pw]
    for _ in range(int(math.log2(CHUNK)) - 1):
        pwb = [a.astype(BF16) for a in pw]
        pw = [_dot(a, a) for a in pwb]
        tinv = [t + _dot(t.astype(BF16), a.astype(BF16)) for t, a in zip(tinv, pw)]
    tinv = [t.astype(BF16) for t in tinv]
    x = [(_dot(tinv[2 * p], cat1([keep(0, at[p]), keep(0, aakv[p])]))
          + _dot(tinv[2 * p + 1], cat1([keep(1, at[p]), keep(1, aakv[p])]))).astype(BF16) for p in pairs]
    y4 = [_dot(cat1(gr[p]).astype(BF16),
               cat0([cat1([keep(0, x[p][:, :LANES]), keep(0, x[p][:, LANES:])]), cat1([zero, keep(0, v[p])]),
                     cat1([keep(1, x[p][:, :LANES]), keep(1, x[p][:, LANES:])]), cat1([zero, keep(1, v[p])])]))
          for p in pairs]
    mn = [_dot_tn(cat0([bb_ref[0, :, sls[p]], kb_ref[0, :, sls[p]]]), cat0([x[p], cat1([zero, v[p]])]))
          for p in pairs]
    lhs = []
    for p in pairs:
        rh = rt[p].astype(F32) + y4[p][:, :LANES]
        mt = jnp.where(eye128, et_ref[0, 0:1, sls[p]], 0.0) + jnp.where(same_head, mn[p][:, :LANES], 0.0)
        lhs.append(cat0([rh, mt]).astype(BF16))
    out = [_dot(lhs[p], ss_scr[p].astype(BF16)) for p in pairs]
    for p in pairs:
        y_ref[0, :, sls[p]] = out[p][:CHUNK] + y4[p][:, LANES:]
        ss_scr[p] = out[p][CHUNK:] + jnp.where(same_head, mn[p][:, LANES:], 0.0)


def _rwkv_scan(at, bt, kt, rt, bb, kb, vb, et, geom, batch):
    rows = vb.shape[0]
    nch = (geom[1] + geom[2]) // CHUNK
    blk = lambda d, b, j: _chunk_block(d, b, j, geom)
    per_dir = pl.BlockSpec((1, CHUNK, MIX_W), lambda d, b, j: (d, blk(d, b, j), 0))
    return pl.pallas_call(
        _rwkv_kernel,
        grid=(2, batch, nch),
        in_specs=[per_dir] * 6 + [
            pl.BlockSpec((CHUNK, MIX_W), lambda d, b, j: (blk(d, b, j), 0)),
            pl.BlockSpec((1, SUBLANES, MIX_W), lambda d, b, j: (d, blk(d, b, j) * (CHUNK // SUBLANES), 0))],
        out_specs=per_dir,
        out_shape=jax.ShapeDtypeStruct((2, rows, MIX_W), F32),
        scratch_shapes=[pltpu.VMEM((R_HEADS // 2, LANES, LANES), F32)],
        compiler_params=pltpu.CompilerParams(dimension_semantics=("arbitrary", "arbitrary", "arbitrary")),
        name='rwkv_scan',
    )(at, bt, kt, rt, bb, kb, vb, et)


def _rpost_kernel(y_ref, rkv_ref, g_ref, lnw_ref, lnb_ref, rk_ref, bd_ref, o_ref):
    y = y_ref[0] + y_ref[1]
    bd = bd_ref[...]
    inv = 1.0 / R_DH
    mu = _dot(y, bd, HI) * inv
    yc = y - mu
    var = _dot(yc * yc, bd, HI) * inv
    yn = yc * lax.rsqrt(var + R_GN_EPS) * lnw_ref[...] + lnb_ref[...]
    r = rkv_ref[:, 0:512]
    k = rkv_ref[:, 512:1024]
    v = rkv_ref[:, 1024:1536]
    bonus = _dot(r * k * rk_ref[...], bd, HI) * v
    o_ref[...] = (yn + bonus) * g_ref[...]


def _block_diag_ones(n, group):
    i = np.arange(n) // group
    return jnp.asarray((i[:, None] == i[None, :]).astype(np.float32))


def _rpost(y_tm, rkv_c, g, ln_w, ln_b, r_k):
    rows = rkv_c.shape[0]
    tm = ROW_TILE
    full = lambda *s: pl.BlockSpec(s, lambda i: (0,) * len(s))
    return pl.pallas_call(
        _rpost_kernel,
        grid=(rows // tm,),
        in_specs=[pl.BlockSpec((2, tm, 512), lambda i: (0, i, 0)),
                  pl.BlockSpec((tm, 1536), lambda i: (i, 0)),
                  pl.BlockSpec((tm, 512), lambda i: (i, 0)),
                  full(1, 512), full(1, 512), full(1, 512), full(512, 512)],
        out_specs=pl.BlockSpec((tm, 512), lambda i: (i, 0)),
        out_shape=jax.ShapeDtypeStruct((rows, 512), F32),
        compiler_params=pltpu.CompilerParams(dimension_semantics=("arbitrary",)),
        name='rwkv_post',
    )(y_tm, rkv_c, g, ln_w.reshape(1, 512), ln_b.reshape(1, 512), r_k.reshape(1, 512),
      _block_diag_ones(512, R_DH))


def _aprep_kernel(q_ref, k_ref, cos_ref, sin_ref, qn_ref, kn_ref, bd_ref, qo_ref, ko_ref):
    bd = bd_ref[...]
    cos = cos_ref[...]
    sin = sin_ref[...]
    lane = lax.broadcasted_iota(jnp.int32, (1, 512), 1)
    lower = lax.rem(lane, A_DH) < A_DH // 2

    def one(x, g):
        ms = _dot(x * x, bd, HI) * (1.0 / A_DH)
        y = x * lax.rsqrt(ms + NORM_EPS) * g
        swapped = jnp.where(lower, pltpu.roll(y, 512 - A_DH // 2, 1), pltpu.roll(y, A_DH // 2, 1))
        return y * cos + swapped * sin

    qo_ref[...] = one(q_ref[...], qn_ref[...]).astype(BF16)
    ko_ref[...] = one(k_ref[...], kn_ref[...]).astype(BF16)


def _rope_tables(seq, tm):
    rows = seq // GRID_W
    row = jnp.repeat(jnp.arange(rows, dtype=F32), GRID_W)
    col = jnp.tile(jnp.arange(GRID_W, dtype=F32), rows)
    half = A_DH // 2
    inv = ROPE_BASE ** (-jnp.arange(0, half, 2, dtype=F32) / half)
    ang = jnp.concatenate([row[:, None] * inv, col[:, None] * inv], axis=-1)
    cos, sin = jnp.cos(ang), jnp.sin(ang)
    cos64 = jnp.concatenate([cos, cos], axis=-1)
    sin64 = jnp.concatenate([-sin, sin], axis=-1)
    cos_t = jnp.concatenate([jnp.tile(cos64, (1, 8)), jnp.ones((tm, 512), F32)], axis=0)
    sin_t = jnp.concatenate([jnp.tile(sin64, (1, 8)), jnp.zeros((tm, 512), F32)], axis=0)
    return cos_t, sin_t


def _aprep(z, cos_t, sin_t, qk_norm, geom):
    rows = z.shape[0]
    nl, seq, _ = geom
    tm = ROW_TILE
    perm = _deinterleave64()
    qn = jnp.tile(qk_norm[0][perm] * (A_DH ** -0.5), 8).reshape(1, 512)
    kn = jnp.tile(qk_norm[1][perm], 8).reshape(1, 512)
    tiles_per_seq = seq // tm
    tab = lambda i: (jnp.where(i < nl // tm, lax.rem(i, tiles_per_seq), tiles_per_seq), 0)
    full = lambda *s: pl.BlockSpec(s, lambda i: (0,) * len(s))
    return pl.pallas_call(
        _aprep_kernel,
        grid=(rows // tm,),
        in_specs=[pl.BlockSpec((tm, 512), lambda i: (i, Z_AQ // 512)),
                  pl.BlockSpec((tm, 512), lambda i: (i, Z_AK // 512)),
                  pl.BlockSpec((tm, 512), tab), pl.BlockSpec((tm, 512), tab),
                  full(1, 512), full(1, 512), full(512, 512)],
        out_specs=[pl.BlockSpec((tm, 512), lambda i: (i, 0)), pl.BlockSpec((tm, 512), lambda i: (i, 0))],
        out_shape=[jax.ShapeDtypeStruct((rows, 512), BF16), jax.ShapeDtypeStruct((rows, 512), BF16)],
        compiler_params=pltpu.CompilerParams(dimension_semantics=("arbitrary",)),
        name='attn_prep',
    )(z, z, cos_t, sin_t, qn, kn, _block_diag_ones(512, A_DH))


def _attn_kernel(*refs, n_lat, lam_init):
    lam_ref, sub_ref, q_ref, kc_ref, vc_ref = refs[:5]
    if n_lat:
        kl_ref, vl_ref = refs[5:7]
    o_ref, m_scr, acc_scr = refs[-3:]

    q = q_ref[...]
    lane = lax.broadcasted_iota(jnp.int32, (1, LANES), 1)
    zero = jnp.zeros_like(q)
    qs = (jnp.where(lane < A_DH, q, zero), jnp.where(lane >= A_DH, q, zero))
    m_scr[...] = jnp.full_like(m_scr, -jnp.inf)
    acc_scr[...] = jnp.zeros_like(acc_scr)

    def step(kb, vt):
        s = [_dot_nt(kb, qm) for qm in qs]
        m_old = [m_scr[i] for i in range(2)]
        m_new = [jnp.maximum(m_old[i], jnp.max(s[i], axis=0, keepdims=True)) for i in range(2)]
        p = [jnp.exp(s[i] - m_new[i]).astype(BF16) for i in range(2)]
        pv = [_dot(vt, p[i]) for i in range(2)]
        for i in range(2):
            acc_scr[i] = jnp.exp(m_old[i] - m_new[i]) * acc_scr[i] + pv[i]
            m_scr[i] = m_new[i]

    step(kc_ref[...], vc_ref[0, 0])
    if n_lat:
        def body(c, carry):
            off = pl.multiple_of(c * KV_TILE, KV_TILE)
            step(kl_ref[pl.ds(off, KV_TILE), :], vl_ref[0, c])
            return carry
        lax.fori_loop(0, n_lat, body, 0)

    lp = lam_ref[...]
    lam = (jnp.exp(jnp.sum(lp[0:1, :] * lp[1:2, :], axis=1, keepdims=True))
           - jnp.exp(jnp.sum(lp[2:3, :] * lp[3:4, :], axis=1, keepdims=True)) + lam_init)
    a0, a1 = acc_scr[0], acc_scr[1]
    o = a0[:A_DV] / a0[A_DV:A_DV + 1] - lam * (a1[:A_DV] / a1[A_DV:A_DV + 1])
    ms = jnp.mean(o * o, axis=0, keepdims=True)
    o = o * lax.rsqrt(ms + NORM_EPS) * sub_ref[...] * (1.0 - lam_init)
    o_ref[...] = o.T


def _augmented_values_t(v, tile):
    n = v.shape[0]
    vt = v.astype(BF16).reshape(n // tile, tile, A_HEADS, A_DV).transpose(2, 0, 3, 1)
    extra = jnp.zeros((A_HEADS, n // tile, V_PAD, tile), BF16).at[:, :, 0, :].set(1.0)
    return jnp.concatenate([vt, extra], axis=2)


def _attention(qr, kr, vt_ctx, vt_lat, lam_p, subln, lam_init, geom, batch, ctx_queries):
    nl, seq, ctx = geom
    ctx_blk = nl // ctx
    full = lambda *s: pl.BlockSpec(s, lambda b, h, i: (0,) * len(s))
    kv_ctx = [pl.BlockSpec((ctx, LANES), lambda b, h, i: (ctx_blk + b, h)),
              pl.BlockSpec((1, 1, A_DV + V_PAD, ctx), lambda b, h, i: (h, b, 0, 0))]
    head = [full(4, A_DH), full(A_DV, 1)]
    if ctx_queries:
        tq, nq, n_lat, out_rows = ctx, 1, 0, batch * ctx
        in_specs = head + [pl.BlockSpec((tq, LANES), lambda b, h, i: (ctx_blk + b, h))] + kv_ctx
        args = (lam_p, subln.reshape(A_DV, 1), qr, kr, vt_ctx)
        omap = lambda b, h, i: (b, h)
    else:
        tq, out_rows = Q_TILE, nl
        nq, n_lat = seq // tq, seq // KV_TILE
        omap = lambda b, h, i: (b * nq + i, h)
        in_specs = head + [pl.BlockSpec((tq, LANES), omap)] + kv_ctx + [
            pl.BlockSpec((seq, LANES), lambda b, h, i: (b, h)),
            pl.BlockSpec((1, n_lat, A_DV + V_PAD, KV_TILE), lambda b, h, i: (h, b, 0, 0))]
        args = (lam_p, subln.reshape(A_DV, 1), qr, kr, vt_ctx, kr, vt_lat)
    return pl.pallas_call(
        functools.partial(_attn_kernel, n_lat=n_lat, lam_init=lam_init),
        grid=(batch, A_HEADS, nq),
        in_specs=in_specs,
        out_specs=pl.BlockSpec((tq, LANES), omap),
        out_shape=jax.ShapeDtypeStruct((out_rows, MIX_W), F32),
        scratch_shapes=[pltpu.VMEM((2, 1, tq), F32), pltpu.VMEM((2, A_DV + V_PAD, tq), F32)],
        compiler_params=pltpu.CompilerParams(dimension_semantics=("arbitrary", "arbitrary", "arbitrary")),
        name='diff_attn_ctx' if ctx_queries else 'diff_attn',
    )(*args)


def _merge_kernel(x_ref, mod_ref, hm_ref, mo_ref, on_ref, yr_ref, ya_ref, g_ref, wb_ref, wo_ref, o_ref):
    hm = hm_ref[0] + hm_ref[1]
    on = on_ref[...]
    parts = []
    for h in range(M_HEADS):
        blk = hm[:, h * M_DV:(h + 1) * M_DV]
        ms = jnp.mean(blk * blk, axis=-1, keepdims=True)
        parts.append(blk * lax.rsqrt(ms + NORM_EPS) * on[:, h * M_DV:(h + 1) * M_DV])
    ym = jnp.concatenate(parts, axis=1) * _sigmoid(mo_ref[...])
    d = x_ref.shape[1]
    zsum = _sigmoid(g_ref[:, 0:d]) * _dot(ym.astype(BF16), wb_ref[0])
    zsum += _sigmoid(g_ref[:, d:2 * d]) * _dot(yr_ref[...].astype(BF16), wb_ref[1])
    zsum += _sigmoid(g_ref[:, 2 * d:3 * d]) * _dot(ya_ref[...].astype(BF16), wb_ref[2])
    o_ref[...] = x_ref[...] + mod_ref[0, 5:6, :] * _dot(zsum.astype(BF16), wo_ref[...])


def _merge(xr, mods, hm, z, out_norm, yr, ya, w_branch, w_o, rows, stream_of_tile):
    d = xr.shape[1]
    tm = ROW_TILE
    full = lambda *s: pl.BlockSpec(s, lambda i: (0,) * len(s))
    return pl.pallas_call(
        _merge_kernel,
        grid=(rows // tm,),
        in_specs=[pl.BlockSpec((tm, d), lambda i: (i, 0)),
                  pl.BlockSpec((1, N_ADA, d), lambda i: (stream_of_tile(i), 0, 0)),
                  pl.BlockSpec((2, tm, MIX_W), lambda i: (0, i, 0)),
                  pl.BlockSpec((tm, MIX_W), lambda i: (i, Z_MO // MIX_W)),
                  full(1, MIX_W),
                  pl.BlockSpec((tm, MIX_W), lambda i: (i, 0)),
                  pl.BlockSpec((tm, MIX_W), lambda i: (i, 0)),
                  pl.BlockSpec((tm, 3 * d), lambda i: (i, Z_G // (3 * d))),
                  full(3, MIX_W, d), full(d, d)],
        out_specs=pl.BlockSpec((tm, d), lambda i: (i, 0)),
        out_shape=jax.ShapeDtypeStruct((rows, d), F32),
        compiler_params=pltpu.CompilerParams(dimension_semantics=("arbitrary",)),
        name='merge',
    )(xr, mods, hm, z, out_norm.reshape(1, MIX_W), yr, ya, z, w_branch, w_o)


def _lambda_init(layer):
    return 0.8 - 0.6 * math.exp(-0.3 * layer)


def kernel(x, c, ctx, c_ctx, w_ada, b_ada, norm_g, ffn1_w_gu, ffn1_w_down, ffn2_w_gu, ffn2_w_down,
           w_in, m_conv, m_gate_bias, m_out_norm, r_conv, r_w0, r_w2, r_a0, r_a2, r_g2, r_kk, r_ka,
           r_rk, r_ln_w, r_ln_b, a_qk_norm, a_lambda, a_subln, w_branch, w_o):
    batch, seq, d = x.shape
    ctx_len = ctx.shape[1]
    depth = w_ada.shape[0]
    nl, nc = batch * seq, batch * ctx_len
    rows = nl + nc
    geom = (nl, seq, ctx_len)
    tm = ROW_TILE
    assert seq % tm == 0 and nc % tm == 0 and seq % KV_TILE == 0 and nl % ctx_len == 0
    assert ctx_len % CHUNK == 0 and seq % GRID_W == 0 and d == 1024

    tiles_per_seq = seq // tm
    stream_of_tile = lambda i: jnp.minimum(i // tiles_per_seq, batch)

    cv = jnp.zeros((SUBLANES, d), F32).at[:batch].set(c).at[batch].set(c_ctx)
    mods_all = _ada_all(cv, w_ada, b_ada)[:, :batch + 1].reshape(depth, batch + 1, N_ADA, d)

    zcols = _z_column_sources()
    zsrc = jnp.asarray(np.maximum(zcols, 0))
    zmask = jnp.asarray((zcols >= 0).astype(np.float32))
    cos_t, sin_t = _rope_tables(seq, tm)

    xr = jnp.concatenate([x.reshape(nl, d), ctx.reshape(nc, d)], axis=0)
    for li in range(depth):
        last = li == depth - 1
        mods = mods_all[li]
        w_z = (jnp.take(w_in[li], zsrc, axis=1) * zmask).astype(BF16)

        xr = _ffn(xr, mods, norm_g[li, 0], ffn1_w_gu[li].astype(BF16), ffn1_w_down[li].astype(BF16),
                  0, rows, stream_of_tile)
        z = _inproj(xr, mods, norm_g[li, 1], w_z, stream_of_tile)

        qk_act, gcol = _mprep(z, m_conv[li], m_gate_bias[li], geom)
        qk_hm = qk_act.reshape(rows, 2 * M_HEADS, M_DQK).transpose(1, 0, 2)
        grow = gcol[:, :16].reshape(rows // CHUNK, CHUNK, 16).transpose(0, 2, 1)
        hm = _mlstm_scan(qk_hm, z, gcol, grow, geom, batch)

        rkv_c, g_r, vb, *scan_in = _rprep(z, r_conv[li], r_w0[li], r_w2[li], r_a0[li], r_a2[li], r_g2[li],
                                           r_kk[li], r_ka[li], geom)
        y_tm = _rwkv_scan(*scan_in[:6], vb, scan_in[6], geom, batch)
        yr = _rpost(y_tm, rkv_c, g_r, r_ln_w[li], r_ln_b[li], r_rk[li])

        qr, kr = _aprep(z, cos_t, sin_t, a_qk_norm[li], geom)
        v_att = z[:, Z_AV:Z_AV + MIX_W]
        vt_ctx = _augmented_values_t(v_att[nl:], ctx_len)
        vt_lat = _augmented_values_t(v_att[:nl], KV_TILE)
        att = lambda cq: _attention(qr, kr, vt_ctx, vt_lat, a_lambda[li], a_subln[li], _lambda_init(li),
                                    geom, batch, cq)
        ya = att(False) if last else jnp.concatenate([att(False), att(True)], axis=0)

        out_rows = nl if last else rows
        xr = _merge(xr, mods, hm, z, m_out_norm[li], yr, ya, w_branch[li].astype(BF16),
                    w_o[li].astype(BF16), out_rows, stream_of_tile)
        xr = _ffn(xr, mods, norm_g[li, 2], ffn2_w_gu[li].astype(BF16), ffn2_w_down[li].astype(BF16),
                  6, out_rows, stream_of_tile)
    return xr[:nl].reshape(batch, seq, d)
```

```python
import functools
import math

import numpy as np
import jax
import jax.numpy as jnp
from jax import lax
from jax.experimental import pallas as pl
from jax.experimental.pallas import tpu as pltpu

F32 = jnp.float32
BF16 = jnp.bfloat16
HI = lax.Precision.HIGHEST

N_ADA = 9
MACARON_W = 0.5
NORM_EPS = 1e-6
MIX_W = 512
GRID_W = 64
CHUNK = 64
SCAN_CHUNKS = 4
M_SCAN_CHUNKS = 2

M_HEADS = 4
M_DQK = 64
M_DV = 128
R_HEADS = 8
R_DH = 64
R_GN_EPS = 64e-5
A_HEADS = 4
A_DH = 64
A_DV = 128
ROPE_BASE = 10000.0

LANES = 128
SUBLANES = 8
ROW_TILE = 512
RWKV_PREP_TILE = 256
Q_TILE = 512
KV_TILE = 512
V_PAD = 16
LOG2_E = math.log2(math.e)
FFN_CHUNK = 256
FFN_VMEM_LIMIT = 48 * 2 ** 20

_IN_SPLITS = (
    ('m_q', 256), ('m_k', 256), ('m_v', 512), ('m_o', 512),
    ('m_if', 4), ('m_ff', 4), ('m_ib', 4), ('m_fb', 4),
    ('r_r', 512), ('r_k', 512), ('r_v', 512),
    ('r_wf', 64), ('r_wb', 64), ('r_af', 64), ('r_ab', 64), ('r_g', 128),
    ('a_q', 512), ('a_k', 512), ('a_v', 512),
    ('g_m', 1024), ('g_r', 1024), ('g_a', 1024),
)
Z_G, Z_MQK, Z_MV, Z_MO, Z_RKV, Z_AQ, Z_AK, Z_AV, Z_LORA, Z_MG, Z_W = (
    0, 3072, 3584, 4096, 4608, 6144, 6656, 7168, 7680, 8064, 8192)
Z_TILE = 1024


def _deinterleave64():
    return np.concatenate([np.arange(0, 64, 2), np.arange(1, 64, 2)])


def _z_column_sources():
    off, start = {}, 0
    for name, w in _IN_SPLITS:
        off[name] = start
        start += w
    cols = []
    rng = lambda n: list(range(off[n], off[n] + dict(_IN_SPLITS)[n]))
    cols += rng('g_m') + rng('g_r') + rng('g_a')
    cols += rng('m_q') + rng('m_k') + rng('m_v') + rng('m_o')
    cols += rng('r_r') + rng('r_k') + rng('r_v')
    perm = _deinterleave64()
    for n in ('a_q', 'a_k'):
        for g in range(8):
            cols += list(off[n] + g * 64 + perm)
    cols += rng('a_v')
    cols += rng('r_wf') + rng('r_wb') + rng('r_af') + rng('r_ab') + rng('r_g')
    cols += rng('m_if') + rng('m_ff') + rng('m_ib') + rng('m_fb')
    cols += [-1] * (Z_W - len(cols))
    assert len(cols) == Z_W
    return np.asarray(cols, np.int32)


def _sigmoid(x):
    return 1.0 / (1.0 + jnp.exp(-x))


def _dot(a, b, prec=None):
    return jnp.dot(a, b, preferred_element_type=F32, precision=prec)


def _dot_nt(a, b, prec=None):
    return lax.dot_general(a, b, (((1,), (1,)), ((), ())), preferred_element_type=F32, precision=prec)


def _dot_tn(a, b, prec=None):
    return lax.dot_general(a, b, (((0,), (0,)), ((), ())), preferred_element_type=F32, precision=prec)


def _norm_mod(x, g, shift, scale):
    ms = jnp.mean(x * x, axis=-1, keepdims=True)
    y = x * lax.rsqrt(ms + NORM_EPS) * g
    return y * (1.0 + scale) + shift


def _ada_kernel(c_ref, w_ref, b_ref, o_ref):
    c = c_ref[...]
    s = c * _sigmoid(c)
    o_ref[0] = _dot(s, w_ref[0], HI) + b_ref[0]


def _ada_all(cv, w_ada, b_ada):
    depth, d, nd = w_ada.shape
    tn = 1024
    return pl.pallas_call(
        _ada_kernel,
        grid=(depth, nd // tn),
        in_specs=[pl.BlockSpec((SUBLANES, d), lambda l, n: (0, 0)),
                  pl.BlockSpec((1, d, tn), lambda l, n: (l, 0, n)),
                  pl.BlockSpec((1, 1, tn), lambda l, n: (l, 0, n))],
        out_specs=pl.BlockSpec((1, SUBLANES, tn), lambda l, n: (l, 0, n)),
        out_shape=jax.ShapeDtypeStruct((depth, SUBLANES, nd), F32),
        name='ada',
    )(cv, w_ada, b_ada.reshape(depth, 1, nd))


def _ffn_kernel(x_ref, mod_ref, g_ref, wgu_ref, wd_ref, o_ref, u_scr, *, mi):
    dff = wd_ref.shape[0]
    h = _norm_mod(x_ref[...], g_ref[...], mod_ref[0, mi:mi + 1, :], mod_ref[0, mi + 1:mi + 2, :]).astype(BF16)
    for j in range(dff // FFN_CHUNK):
        lo, hi = j * FFN_CHUNK, (j + 1) * FFN_CHUNK
        a = _dot(h, wgu_ref[:, lo:hi])
        b = _dot(h, wgu_ref[:, dff + lo:dff + hi])
        u_scr[:, lo:hi] = ((a * _sigmoid(a)) * b).astype(BF16)
    o_ref[...] = x_ref[...] + MACARON_W * mod_ref[0, mi + 2:mi + 3, :] * _dot(u_scr[...], wd_ref[...])


def _ffn(xr, mods, g, w_gu, w_down, mi, rows, stream_of_tile):
    d = xr.shape[1]
    dff = w_down.shape[0]
    assert dff % FFN_CHUNK == 0
    tm = ROW_TILE
    resident = lambda *s: pl.BlockSpec(s, lambda i: (0,) * len(s), pipeline_mode=pl.Buffered(1))
    return pl.pallas_call(
        functools.partial(_ffn_kernel, mi=mi),
        grid=(rows // tm,),
        in_specs=[pl.BlockSpec((tm, d), lambda i: (i, 0)),
                  pl.BlockSpec((1, N_ADA, d), lambda i: (stream_of_tile(i), 0, 0)),
                  pl.BlockSpec((1, d), lambda i: (0, 0)),
                  resident(d, 2 * dff), resident(dff, d)],
        out_specs=pl.BlockSpec((tm, d), lambda i: (i, 0)),
        out_shape=jax.ShapeDtypeStruct((rows, d), F32),
        scratch_shapes=[pltpu.VMEM((tm, dff), BF16)],
        compiler_params=pltpu.CompilerParams(dimension_semantics=("arbitrary",),
                                             vmem_limit_bytes=FFN_VMEM_LIMIT),
        name='ffn',
    )(xr, mods, g.reshape(1, d), w_gu, w_down)


def _inproj_kernel(x_ref, mod_ref, g_ref, w_ref, o_ref, h_scr):
    @pl.when(pl.program_id(1) == 0)
    def _():
        h = _norm_mod(x_ref[...], g_ref[...], mod_ref[0, 3:4, :], mod_ref[0, 4:5, :])
        h_scr[...] = h.astype(BF16)

    o_ref[...] = _dot(h_scr[...], w_ref[pl.program_id(1)])


def _inproj(xr, mods, g, w_z, stream_of_tile):
    rows, d = xr.shape
    tm = ROW_TILE
    nz = Z_W // Z_TILE
    return pl.pallas_call(
        _inproj_kernel,
        grid=(rows // tm, nz),
        in_specs=[pl.BlockSpec((tm, d), lambda i, n: (i, 0)),
                  pl.BlockSpec((1, N_ADA, d), lambda i, n: (stream_of_tile(i), 0, 0)),
                  pl.BlockSpec((1, d), lambda i, n: (0, 0)),
                  pl.BlockSpec((nz, d, Z_TILE), lambda i, n: (0, 0, 0), pipeline_mode=pl.Buffered(1))],
        out_specs=pl.BlockSpec((tm, Z_TILE), lambda i, n: (i, n)),
        out_shape=jax.ShapeDtypeStruct((rows, Z_W), F32),
        scratch_shapes=[pltpu.VMEM((tm, d), BF16)],
        compiler_params=pltpu.CompilerParams(dimension_semantics=("arbitrary", "arbitrary"),
                                             vmem_limit_bytes=FFN_VMEM_LIMIT),
        name='inproj',
    )(xr, mods, g.reshape(1, d), w_z)


def _conv3(x, prev8, next8, w, row0, geom):
    nl, seq, ctx = geom
    tm = x.shape[0]
    t = lax.broadcasted_iota(jnp.int32, (tm, 1), 0)
    r = row0 + t
    is_lat = r < nl
    pos = jnp.where(is_lat, lax.rem(r, seq), lax.rem(jnp.maximum(r - nl, 0), ctx))
    seglen = jnp.where(is_lat, seq, ctx)
    xm = pltpu.roll(x, 1, 0)
    xm = jnp.where(t == 0, prev8[SUBLANES - 1:SUBLANES, :], xm)
    xm = jnp.where(pos == 0, 0.0, xm)
    xp = pltpu.roll(x, tm - 1, 0)
    xp = jnp.where(t == tm - 1, next8[0:1, :], xp)
    xp = jnp.where(pos == seglen - 1, 0.0, xp)
    return xm * w[0:1, :] + x * w[1:2, :] + xp * w[2:3, :]


def _halo_specs(tm, width, col_blk, rows):
    per = tm // SUBLANES
    last = rows // SUBLANES - 1
    return [pl.BlockSpec((tm, width), lambda i: (i, col_blk)),
            pl.BlockSpec((SUBLANES, width), lambda i: (jnp.maximum(i * per - 1, 0), col_blk)),
            pl.BlockSpec((SUBLANES, width), lambda i: (jnp.minimum((i + 1) * per, last), col_blk))]


def _mprep_kernel(x_ref, xp_ref, xn_ref, w_ref, gt_ref, gb_ref, qk_ref, go_ref, *, geom):
    tm = x_ref.shape[0]
    y = _conv3(x_ref[...], xp_ref[...], xn_ref[...], w_ref[...], pl.program_id(0) * tm, geom)
    y = y * _sigmoid(y)
    lane = lax.broadcasted_iota(jnp.int32, (1, y.shape[1]), 1)
    qk_ref[...] = jnp.where(lane >= M_HEADS * M_DQK, y * (M_DQK ** -0.5), y)
    g = gt_ref[...] + gb_ref[...]
    gl = lax.broadcasted_iota(jnp.int32, (1, LANES), 1)
    is_forget = (lax.rem(gl, 2 * M_HEADS) >= M_HEADS) & (gl < 4 * M_HEADS)
    logsig = jnp.minimum(g, 0.0) - jnp.log(1.0 + jnp.exp(-jnp.abs(g)))
    go_ref[...] = jnp.where(is_forget, logsig, g)


def _mprep(z, conv_w, gate_bias, geom):
    rows = z.shape[0]
    tm = ROW_TILE
    gb = jnp.zeros((1, LANES), F32).at[0, :4 * M_HEADS].set(gate_bias.reshape(-1))
    return pl.pallas_call(
        functools.partial(_mprep_kernel, geom=geom),
        grid=(rows // tm,),
        in_specs=_halo_specs(tm, 512, Z_MQK // 512, rows) + [
            pl.BlockSpec((3, 512), lambda i: (0, 0)),
            pl.BlockSpec((tm, LANES), lambda i: (i, Z_MG // LANES)),
            pl.BlockSpec((1, LANES), lambda i: (0, 0))],
        out_specs=[pl.BlockSpec((tm, 512), lambda i: (i, 0)),
                   pl.BlockSpec((tm, LANES), lambda i: (i, 0))],
        out_shape=[jax.ShapeDtypeStruct((rows, 512), F32), jax.ShapeDtypeStruct((rows, LANES), F32)],
        compiler_params=pltpu.CompilerParams(dimension_semantics=("arbitrary",)),
        name='mlstm_prep',
    )(z, z, z, conv_w, z, gb)


def _chunk_block(d, b, j, geom, size=CHUNK):
    nl, seq, ctx = geom
    nc_c, nc_l = ctx // size, seq // size
    jl = j - nc_c
    c_ctx = jnp.where(d == 0, j, nc_c - 1 - j)
    c_lat = jnp.where(d == 0, jl, nc_l - 1 - jl)
    return jnp.where(j < nc_c, nl // size + b * nc_c + c_ctx, b * nc_l + c_lat)


def _order_masks(d):
    ti = lax.broadcasted_iota(jnp.int32, (CHUNK, CHUNK), 0)
    si = lax.broadcasted_iota(jnp.int32, (CHUNK, CHUNK), 1)
    fwd = d == 0
    ahead = jnp.where(fwd, si - ti, ti - si)
    incl = ahead <= 0
    strict = ahead < 0
    incl_t = ahead >= 0
    return fwd, incl, strict, incl_t, ti == si


def _mlstm_kernel(qk_ref, v_ref, gc_ref, gr_ref, o_ref, ct_scr, m_scr):
    d = pl.program_id(0)

    @pl.when(pl.program_id(2) == 0)
    def _():
        ct_scr[...] = jnp.zeros_like(ct_scr)
        m_scr[...] = jnp.zeros_like(m_scr)

    fwd, incl, _, incl_t, _ = _order_masks(d)
    heads = range(M_HEADS)
    chunks = range(M_SCAN_CHUNKS)
    lane = lax.broadcasted_iota(jnp.int32, (CHUNK, LANES), 1)
    ones_col = jnp.where(lane == 0, 1.0, 0.0)
    cat1 = lambda xs: jnp.concatenate(xs, axis=1)

    pre = []
    for u in chunks:
        cu = jnp.where(fwd, u, M_SCAN_CHUNKS - 1 - u)
        rows = pl.ds(pl.multiple_of(cu * CHUNK, CHUNK), CHUNK)
        gc = gc_ref[rows, :]
        gr = gr_ref[cu]
        q = [qk_ref[h, rows, :].astype(BF16) for h in heads]
        k = [qk_ref[M_HEADS + h, rows, :].astype(BF16) for h in heads]
        v = [v_ref[rows, h * M_DV:(h + 1) * M_DV] for h in heads]
        qk = [_dot_nt(q[h], k[h]) for h in heads]
        per_head = []
        for h in heads:
            i_col = jnp.where(fwd, gc[:, h:h + 1], gc[:, 8 + h:9 + h])
            f_col = jnp.where(fwd, gc[:, 4 + h:5 + h], gc[:, 12 + h:13 + h])
            i_row = jnp.where(fwd, gr[h:h + 1, :], gr[8 + h:9 + h, :])
            f_row = jnp.where(fwd, gr[4 + h:5 + h, :], gr[12 + h:13 + h, :])
            bcum_col = jnp.sum(jnp.where(incl, f_row, 0.0), axis=1, keepdims=True)
            bcum_row = jnp.sum(jnp.where(incl_t, f_col, 0.0), axis=0, keepdims=True)
            dlog = jnp.where(incl, bcum_col - bcum_row + i_row, -jnp.inf)
            rmax = jnp.max(dlog, axis=1, keepdims=True)
            btot = jnp.sum(f_col, axis=0, keepdims=True)
            wlog = btot - bcum_col + i_col
            per_head.append(dict(bcum=bcum_col, rmax=rmax, btot=btot, wlog=wlog,
                                 wmax=jnp.max(wlog, axis=0, keepdims=True),
                                 qkd=(qk[h] * jnp.exp(dlog - rmax)).astype(BF16)))
        v_aug = [cat1([v[h], ones_col]) for h in heads]
        sv0 = [_dot(per_head[h]['qkd'], v_aug[h].astype(BF16)) for h in heads]
        pre.append(dict(rows=rows, q=q, k=k, v_aug=v_aug, sv0=sv0, g=per_head))

    ct = [ct_scr[h] for h in heads]
    m_st = [m_scr[h] for h in heads]
    for u in chunks:
        c = pre[u]
        m_new = [jnp.maximum(c['g'][h]['btot'] + m_st[h], c['g'][h]['wmax']) for h in heads]
        ws = [jnp.exp(c['g'][h]['wlog'] - m_new[h]) for h in heads]
        kv = [_dot_tn(c['k'][h], (ws[h] * c['v_aug'][h]).astype(BF16)) for h in heads]
        qc = [_dot(c['q'][h], ct[h].astype(BF16)) for h in heads]
        for h in heads:
            g = c['g'][h]
            inter = g['bcum'] + m_st[h]
            mt = jnp.maximum(inter, g['rmax'])
            nd = jnp.exp(g['rmax'] - mt) * c['sv0'][h] + jnp.exp(inter - mt) * qc[h]
            den = jnp.maximum(jnp.abs(nd[:, M_DV:M_DV + 1]), jnp.exp(-mt))
            o_ref[0, c['rows'], h * M_DV:(h + 1) * M_DV] = nd[:, :M_DV] / den
            ct[h] = jnp.exp(g['btot'] + m_st[h] - m_new[h]) * ct[h] + kv[h]
            m_st[h] = m_new[h]
    for h in heads:
        ct_scr[h] = ct[h]
        m_scr[h] = m_st[h]


def _mlstm_scan(qk_hm, z, gcol, grow, geom, batch):
    rows = z.shape[0]
    step = M_SCAN_CHUNKS * CHUNK
    assert geom[1] % step == 0 and geom[2] % step == 0
    nch = (geom[1] + geom[2]) // step
    blk = lambda d, b, j: _chunk_block(d, b, j, geom, step)
    return pl.pallas_call(
        _mlstm_kernel,
        grid=(2, batch, nch),
        in_specs=[pl.BlockSpec((2 * M_HEADS, step, M_DQK), lambda d, b, j: (0, blk(d, b, j), 0)),
                  pl.BlockSpec((step, MIX_W), lambda d, b, j: (blk(d, b, j), Z_MV // MIX_W)),
                  pl.BlockSpec((step, LANES), lambda d, b, j: (blk(d, b, j), 0)),
                  pl.BlockSpec((M_SCAN_CHUNKS, 16, CHUNK), lambda d, b, j: (blk(d, b, j), 0, 0))],
        out_specs=pl.BlockSpec((1, step, MIX_W), lambda d, b, j: (d, blk(d, b, j), 0)),
        out_shape=jax.ShapeDtypeStruct((2, rows, MIX_W), F32),
        scratch_shapes=[pltpu.VMEM((M_HEADS, M_DQK, M_DV + LANES), F32),
                        pltpu.VMEM((M_HEADS, 1, 1), F32)],
        compiler_params=pltpu.CompilerParams(dimension_semantics=("arbitrary", "arbitrary", "arbitrary")),
        name='mlstm_scan',
    )(qk_hm, z, gcol, grow)


def _rprep_kernel(x_ref, xp_ref, xn_ref, cw_ref, lo_ref, w0_ref, w2_ref, a0_ref, a2_ref, g2_ref,
                  kkw_ref, kaw_ref, bd_ref, tri_ref,
                  rkv_ref, g_ref, vb_ref, at_ref, bt_ref, kt_ref, rt_ref, bb_ref, kb_ref, et_ref, *, geom):
    tm = x_ref.shape[0]
    rkv = _conv3(x_ref[...], xp_ref[...], xn_ref[...], cw_ref[...], pl.program_id(0) * tm, geom)
    rkv_ref[...] = rkv
    r = rkv[:, 0:MIX_W]
    k = rkv[:, MIX_W:2 * MIX_W]
    vb_ref[...] = rkv[:, 2 * MIX_W:3 * MIX_W].astype(BF16)
    lo = lo_ref[...]
    g_ref[...] = _dot(_sigmoid(lo[:, 256:384]), g2_ref[...], HI)
    kkr = k * kkw_ref[...]
    kk = kkr / jnp.maximum(jnp.sqrt(_dot(kkr * kkr, bd_ref[...], HI)), 1e-12)
    for d in range(2):
        w_raw = w0_ref[d:d + 1, :] + _dot(jnp.tanh(lo[:, 64 * d:64 * d + 64]), w2_ref[d], HI)
        lw = -_sigmoid(w_raw) * math.exp(-0.5)
        a = _sigmoid(a0_ref[d:d + 1, :] + _dot(lo[:, 128 + 64 * d:192 + 64 * d], a2_ref[d], HI))
        hi = lw.astype(BF16)
        rem = lw - hi.astype(F32)
        mid = rem.astype(BF16)
        low = (rem - mid.astype(F32)).astype(BF16)
        parts = jnp.concatenate([hi, mid, low], axis=1)
        c3 = _dot(tri_ref[d, 0], parts)
        s3 = _dot(tri_ref[d, 1], parts)
        cum = c3[:, 0:MIX_W] + c3[:, MIX_W:2 * MIX_W] + c3[:, 2 * MIX_W:3 * MIX_W]
        suf = s3[:, 0:MIX_W] + s3[:, MIX_W:2 * MIX_W] + s3[:, 2 * MIX_W:3 * MIX_W]
        kd = k * (1.0 + (a - 1.0) * kaw_ref[...])
        kka = kk * a
        e_neg = jnp.exp(-cum)
        e_end = jnp.exp(suf)
        at_ref[d] = (-kk * jnp.exp(cum - lw)).astype(BF16)
        bt_ref[d] = (kka * e_neg).astype(BF16)
        kt_ref[d] = (kd * e_neg).astype(BF16)
        rt_ref[d] = (r * jnp.exp(cum)).astype(BF16)
        bb_ref[d] = (kka * e_end).astype(BF16)
        kb_ref[d] = (kd * e_end).astype(BF16)
        et_ref[d] = jnp.exp(cum + suf)


def _chunk_order_matrices(tm):
    i = np.arange(tm)
    same = (i[:, None] // CHUNK) == (i[None, :] // CHUNK)
    le = i[None, :] <= i[:, None]
    ge = i[None, :] >= i[:, None]
    mats = np.stack([np.stack([same & le, same & ~le]), np.stack([same & ge, same & ~ge])])
    return jnp.asarray(mats.astype(np.float32), dtype=BF16)


def _rprep(z, conv_w, w0, w2, a0, a2, g2, kk_w, ka_w, geom):
    rows = z.shape[0]
    tm = RWKV_PREP_TILE
    full = lambda *s: pl.BlockSpec(s, lambda i: (0,) * len(s))
    row = lambda w: pl.BlockSpec((tm, w), lambda i: (i, 0))
    both = pl.BlockSpec((2, tm, MIX_W), lambda i: (0, i, 0))
    shp = lambda dt: jax.ShapeDtypeStruct((2, rows, MIX_W), dt)
    return pl.pallas_call(
        functools.partial(_rprep_kernel, geom=geom),
        grid=(rows // tm,),
        in_specs=_halo_specs(tm, 1536, Z_RKV // 1536, rows) + [
            full(3, 1536),
            pl.BlockSpec((tm, 384), lambda i: (i, Z_LORA // 384)),
            full(2, 512), full(2, 64, 512), full(2, 512), full(2, 64, 512), full(128, 512),
            full(1, MIX_W), full(1, MIX_W), full(MIX_W, MIX_W), full(2, 2, tm, tm)],
        out_specs=[row(1536), row(MIX_W), row(MIX_W)] + [both] * 7,
        out_shape=[jax.ShapeDtypeStruct((rows, 1536), F32), jax.ShapeDtypeStruct((rows, MIX_W), F32),
                   jax.ShapeDtypeStruct((rows, MIX_W), BF16)] + [shp(BF16)] * 6 + [shp(F32)],
        compiler_params=pltpu.CompilerParams(dimension_semantics=("arbitrary",)),
        name='rwkv_prep',
    )(z, z, z, conv_w, z, w0, w2, a0, a2, g2, kk_w.reshape(1, MIX_W), ka_w.reshape(1, MIX_W),
      _block_diag_ones(MIX_W, R_DH), _chunk_order_matrices(tm))


def _rwkv_kernel(at_ref, bt_ref, kt_ref, rt_ref, bb_ref, kb_ref, v_ref, et_ref, y_ref, ss_scr):
    d = pl.program_id(0)

    @pl.when(pl.program_id(2) == 0)
    def _():
        ss_scr[...] = jnp.zeros_like(ss_scr)

    ti = lax.broadcasted_iota(jnp.int32, (CHUNK, LANES), 0)
    li = lax.broadcasted_iota(jnp.int32, (CHUNK, LANES), 1)
    si = jnp.bitwise_and(li, R_DH - 1)
    ahead = jnp.where(d == 0, si - ti, ti - si)
    strict2 = ahead < 0
    incl2 = ahead <= 0
    first = li < R_DH
    eye64 = (lax.broadcasted_iota(jnp.int32, (CHUNK, R_DH), 0)
             == lax.broadcasted_iota(jnp.int32, (CHUNK, R_DH), 1)).astype(F32)
    rr = lax.broadcasted_iota(jnp.int32, (LANES, LANES), 0)
    cc = lax.broadcasted_iota(jnp.int32, (LANES, LANES), 1)
    same_head = (rr < R_DH) == (cc < R_DH)
    eye128 = rr == cc
    zero = jnp.zeros((CHUNK, LANES), BF16)
    cat0 = lambda xs: jnp.concatenate(xs, axis=0)
    cat1 = lambda xs: jnp.concatenate(xs, axis=1)
    keep = lambda h, x: jnp.where(first, x, zero) if h == 0 else jnp.where(first, zero, x)

    pairs = range(R_HEADS // 2)
    items = [(u, p) for u in range(SCAN_CHUNKS) for p in pairs]
    rows_of = [pl.ds(pl.multiple_of(jnp.where(d == 0, u, SCAN_CHUNKS - 1 - u) * CHUNK, CHUNK), CHUNK)
               for u in range(SCAN_CHUNKS)]
    lanes_of = [slice(p * LANES, (p + 1) * LANES) for p in pairs]
    at = [at_ref[0, rows_of[u], lanes_of[p]] for u, p in items]
    rt = [rt_ref[0, rows_of[u], lanes_of[p]] for u, p in items]
    v = [v_ref[rows_of[u], lanes_of[p]] for u, p in items]
    n = range(len(items))
    g = [_dot_nt(cat0([keep(0, at[i]), keep(0, rt[i]), keep(1, at[i]), keep(1, rt[i])]),
                 cat0([bt_ref[0, rows_of[u], lanes_of[p]], kt_ref[0, rows_of[u], lanes_of[p]]]))
         for i, (u, p) in enumerate(items)]
    ga = [[jnp.where(strict2, g[i][2 * h * CHUNK:(2 * h + 1) * CHUNK], 0.0) for h in range(2)] for i in n]
    gr = [[jnp.where(incl2, g[i][(2 * h + 1) * CHUNK:(2 * h + 2) * CHUNK], 0.0) for h in range(2)] for i in n]
    aakv = [_dot(cat1(ga[i]).astype(BF16), cat0([zero, keep(0, v[i]), zero, keep(1, v[i])])).astype(BF16)
            for i in n]
    pw = [ga[i][h][:, :R_DH] for i in n for h in range(2)]
    tinv = [eye64 + a for a in pw]
    for _ in range(int(math.log2(CHUNK)) - 1):
        pwb = [a.astype(BF16) for a in pw]
        pw = [_dot(a, a) for a in pwb]
        tinv = [t + _dot(t.astype(BF16), a.astype(BF16)) for t, a in zip(tinv, pw)]
    tinv = [t.astype(BF16) for t in tinv]
    x = [(_dot(tinv[2 * i], cat1([keep(0, at[i]), keep(0, aakv[i])]))
          + _dot(tinv[2 * i + 1], cat1([keep(1, at[i]), keep(1, aakv[i])]))).astype(BF16) for i in n]
    y4 = [_dot(cat1(gr[i]).astype(BF16),
               cat0([cat1([keep(0, x[i][:, :LANES]), keep(0, x[i][:, LANES:])]), cat1([zero, keep(0, v[i])]),
                     cat1([keep(1, x[i][:, :LANES]), keep(1, x[i][:, LANES:])]), cat1([zero, keep(1, v[i])])]))
          for i in n]
    mn = [_dot_tn(cat0([bb_ref[0, rows_of[u], lanes_of[p]], kb_ref[0, rows_of[u], lanes_of[p]]]),
                  cat0([x[i], cat1([zero, v[i]])])) for i, (u, p) in enumerate(items)]
    lhs = []
    for i, (u, p) in enumerate(items):
        rh = rt[i].astype(F32) + y4[i][:, :LANES]
        decay = et_ref[0, pl.ds(jnp.where(d == 0, u, SCAN_CHUNKS - 1 - u) * CHUNK, 1), lanes_of[p]]
        mt = jnp.where(eye128, decay, 0.0) + jnp.where(same_head, mn[i][:, :LANES], 0.0)
        lhs.append(cat0([rh, mt]).astype(BF16))
    ss = [ss_scr[p] for p in pairs]
    for u in range(SCAN_CHUNKS):
        out = [_dot(lhs[u * len(pairs) + p], ss[p].astype(BF16)) for p in pairs]
        for p in pairs:
            i = u * len(pairs) + p
            y_ref[0, rows_of[u], lanes_of[p]] = out[p][:CHUNK] + y4[i][:, LANES:]
            ss[p] = out[p][CHUNK:] + jnp.where(same_head, mn[i][:, LANES:], 0.0)
    for p in pairs:
        ss_scr[p] = ss[p]


def _rwkv_scan(at, bt, kt, rt, bb, kb, vb, et, geom, batch):
    rows = vb.shape[0]
    step = SCAN_CHUNKS * CHUNK
    assert geom[1] % step == 0 and geom[2] % step == 0
    nch = (geom[1] + geom[2]) // step
    blk = lambda d, b, j: _chunk_block(d, b, j, geom, step)
    per_dir = pl.BlockSpec((1, step, MIX_W), lambda d, b, j: (d, blk(d, b, j), 0))
    return pl.pallas_call(
        _rwkv_kernel,
        grid=(2, batch, nch),
        in_specs=[per_dir] * 6 + [pl.BlockSpec((step, MIX_W), lambda d, b, j: (blk(d, b, j), 0)), per_dir],
        out_specs=per_dir,
        out_shape=jax.ShapeDtypeStruct((2, rows, MIX_W), F32),
        scratch_shapes=[pltpu.VMEM((R_HEADS // 2, LANES, LANES), F32)],
        compiler_params=pltpu.CompilerParams(dimension_semantics=("arbitrary", "arbitrary", "arbitrary")),
        name='rwkv_scan',
    )(at, bt, kt, rt, bb, kb, vb, et)


def _rpost_kernel(y_ref, rkv_ref, g_ref, lnw_ref, lnb_ref, rk_ref, bd_ref, o_ref):
    y = y_ref[0] + y_ref[1]
    bd = bd_ref[...]
    inv = 1.0 / R_DH
    mu = _dot(y, bd, HI) * inv
    yc = y - mu
    var = _dot(yc * yc, bd, HI) * inv
    yn = yc * lax.rsqrt(var + R_GN_EPS) * lnw_ref[...] + lnb_ref[...]
    r = rkv_ref[:, 0:512]
    k = rkv_ref[:, 512:1024]
    v = rkv_ref[:, 1024:1536]
    bonus = _dot(r * k * rk_ref[...], bd, HI) * v
    o_ref[...] = (yn + bonus) * g_ref[...]


def _block_diag_ones(n, group):
    i = np.arange(n) // group
    return jnp.asarray((i[:, None] == i[None, :]).astype(np.float32))


def _rpost(y_tm, rkv_c, g, ln_w, ln_b, r_k):
    rows = rkv_c.shape[0]
    tm = ROW_TILE
    full = lambda *s: pl.BlockSpec(s, lambda i: (0,) * len(s))
    return pl.pallas_call(
        _rpost_kernel,
        grid=(rows // tm,),
        in_specs=[pl.BlockSpec((2, tm, 512), lambda i: (0, i, 0)),
                  pl.BlockSpec((tm, 1536), lambda i: (i, 0)),
                  pl.BlockSpec((tm, 512), lambda i: (i, 0)),
                  full(1, 512), full(1, 512), full(1, 512), full(512, 512)],
        out_specs=pl.BlockSpec((tm, 512), lambda i: (i, 0)),
        out_shape=jax.ShapeDtypeStruct((rows, 512), F32),
        compiler_params=pltpu.CompilerParams(dimension_semantics=("arbitrary",)),
        name='rwkv_post',
    )(y_tm, rkv_c, g, ln_w.reshape(1, 512), ln_b.reshape(1, 512), r_k.reshape(1, 512),
      _block_diag_ones(512, R_DH))


def _aprep_kernel(q_ref, k_ref, v_ref, cos_ref, sin_ref, qn_ref, kn_ref, bd_ref, qo_ref, ko_ref, vt_ref):
    tm = v_ref.shape[0]
    pad_row = lax.broadcasted_iota(jnp.int32, (V_PAD, tm), 0)
    pad = jnp.where(pad_row == 0, 1.0, 0.0).astype(BF16)
    for h in range(A_HEADS):
        vt_ref[h, 0, 0:A_DV, :] = v_ref[:, h * A_DV:(h + 1) * A_DV].T.astype(BF16)
        vt_ref[h, 0, A_DV:A_DV + V_PAD, :] = pad
    bd = bd_ref[...]
    cos = cos_ref[...]
    sin = sin_ref[...]
    lane = lax.broadcasted_iota(jnp.int32, (1, 512), 1)
    lower = lax.rem(lane, A_DH) < A_DH // 2

    def one(x, g):
        ms = _dot(x * x, bd, HI) * (1.0 / A_DH)
        y = x * lax.rsqrt(ms + NORM_EPS) * g
        swapped = jnp.where(lower, pltpu.roll(y, 512 - A_DH // 2, 1), pltpu.roll(y, A_DH // 2, 1))
        return y * cos + swapped * sin

    qo_ref[...] = one(q_ref[...], qn_ref[...]).astype(BF16)
    ko_ref[...] = one(k_ref[...], kn_ref[...]).astype(BF16)


def _rope_tables(seq, tm):
    rows = seq // GRID_W
    row = jnp.repeat(jnp.arange(rows, dtype=F32), GRID_W)
    col = jnp.tile(jnp.arange(GRID_W, dtype=F32), rows)
    half = A_DH // 2
    inv = ROPE_BASE ** (-jnp.arange(0, half, 2, dtype=F32) / half)
    ang = jnp.concatenate([row[:, None] * inv, col[:, None] * inv], axis=-1)
    cos, sin = jnp.cos(ang), jnp.sin(ang)
    cos64 = jnp.concatenate([cos, cos], axis=-1)
    sin64 = jnp.concatenate([-sin, sin], axis=-1)
    cos_t = jnp.concatenate([jnp.tile(cos64, (1, 8)), jnp.ones((tm, 512), F32)], axis=0)
    sin_t = jnp.concatenate([jnp.tile(sin64, (1, 8)), jnp.zeros((tm, 512), F32)], axis=0)
    return cos_t, sin_t


def _aprep(z, cos_t, sin_t, qk_norm, geom):
    rows = z.shape[0]
    nl, seq, _ = geom
    tm = ROW_TILE
    perm = _deinterleave64()
    qn = jnp.tile(qk_norm[0][perm] * (A_DH ** -0.5 * LOG2_E), 8).reshape(1, 512)
    kn = jnp.tile(qk_norm[1][perm], 8).reshape(1, 512)
    tiles_per_seq = seq // tm
    tab = lambda i: (jnp.where(i < nl // tm, lax.rem(i, tiles_per_seq), tiles_per_seq), 0)
    full = lambda *s: pl.BlockSpec(s, lambda i: (0,) * len(s))
    return pl.pallas_call(
        _aprep_kernel,
        grid=(rows // tm,),
        in_specs=[pl.BlockSpec((tm, 512), lambda i: (i, Z_AQ // 512)),
                  pl.BlockSpec((tm, 512), lambda i: (i, Z_AK // 512)),
                  pl.BlockSpec((tm, 512), lambda i: (i, Z_AV // 512)),
                  pl.BlockSpec((tm, 512), tab), pl.BlockSpec((tm, 512), tab),
                  full(1, 512), full(1, 512), full(512, 512)],
        out_specs=[pl.BlockSpec((tm, 512), lambda i: (i, 0)), pl.BlockSpec((tm, 512), lambda i: (i, 0)),
                   pl.BlockSpec((A_HEADS, 1, A_DV + V_PAD, tm), lambda i: (0, i, 0, 0))],
        out_shape=[jax.ShapeDtypeStruct((rows, 512), BF16), jax.ShapeDtypeStruct((rows, 512), BF16),
                   jax.ShapeDtypeStruct((A_HEADS, rows // tm, A_DV + V_PAD, tm), BF16)],
        compiler_params=pltpu.CompilerParams(dimension_semantics=("arbitrary",)),
        name='attn_prep',
    )(z, z, z, cos_t, sin_t, qn, kn, _block_diag_ones(512, A_DH))


def _attn_kernel(*refs, n_lat, lam_init):
    lam_ref, sub_ref, q_ref, kc_ref, vc_ref = refs[:5]
    if n_lat:
        kl_ref, vl_ref = refs[5:7]
        o_ref, m_scr, acc_scr, sa_scr, sb_scr = refs[7:]
    else:
        o_ref, m_scr, acc_scr = refs[5:]

    q = q_ref[...]
    lane = lax.broadcasted_iota(jnp.int32, (1, LANES), 1)
    zero = jnp.zeros_like(q)
    qs = (jnp.where(lane < A_DH, q, zero), jnp.where(lane >= A_DH, q, zero))
    m_scr[...] = jnp.full_like(m_scr, -jnp.inf)
    acc_scr[...] = jnp.zeros_like(acc_scr)

    def scores(kb):
        return [_dot_nt(kb, qm) for qm in qs]

    def absorb(s, vt):
        m_old = [m_scr[i] for i in range(2)]
        m_new = [jnp.maximum(m_old[i], jnp.max(s[i], axis=0, keepdims=True)) for i in range(2)]
        p = [jnp.exp2(s[i] - m_new[i]).astype(BF16) for i in range(2)]
        pv = [_dot(vt, p[i]) for i in range(2)]
        for i in range(2):
            acc_scr[i] = jnp.exp2(m_old[i] - m_new[i]) * acc_scr[i] + pv[i]
            m_scr[i] = m_new[i]

    absorb(scores(kc_ref[...]), vc_ref[0, 0])
    if n_lat:
        def keys(c):
            return kl_ref[pl.ds(pl.multiple_of(c * KV_TILE, KV_TILE), KV_TILE), :]

        def put(scr, s):
            scr[0] = s[0]
            scr[1] = s[1]

        put(sa_scr, scores(keys(0)))

        def body(j, carry):
            c = 2 * j
            put(sb_scr, scores(keys(c + 1)))
            absorb([sa_scr[0], sa_scr[1]], vl_ref[0, c])
            put(sa_scr, scores(keys(jnp.minimum(c + 2, n_lat - 1))))
            absorb([sb_scr[0], sb_scr[1]], vl_ref[0, c + 1])
            return carry
        lax.fori_loop(0, n_lat // 2, body, 0)

    lp = lam_ref[...]
    lam = (jnp.exp(jnp.sum(lp[0:1, :] * lp[1:2, :], axis=1, keepdims=True))
           - jnp.exp(jnp.sum(lp[2:3, :] * lp[3:4, :], axis=1, keepdims=True)) + lam_init)
    a0, a1 = acc_scr[0], acc_scr[1]
    o = a0[:A_DV] / a0[A_DV:A_DV + 1] - lam * (a1[:A_DV] / a1[A_DV:A_DV + 1])
    ms = jnp.mean(o * o, axis=0, keepdims=True)
    o = o * lax.rsqrt(ms + NORM_EPS) * sub_ref[...] * (1.0 - lam_init)
    o_ref[...] = o.T


def _attention(qr, kr, vt, lam_p, subln, lam_init, geom, batch, ctx_queries):
    nl, seq, ctx = geom
    ctx_blk = nl // ctx
    per_tile = KV_TILE // ctx
    full = lambda *s: pl.BlockSpec(s, lambda b, h, i: (0,) * len(s))
    kv_ctx = [pl.BlockSpec((ctx, LANES), lambda b, h, i: (ctx_blk + b, h)),
              pl.BlockSpec((1, 1, A_DV + V_PAD, ctx),
                           lambda b, h, i: (h, nl // KV_TILE + b // per_tile, 0, lax.rem(b, per_tile)))]
    head = [full(4, A_DH), full(A_DV, 1)]
    scratch = lambda tq: [pltpu.VMEM((2, 1, tq), F32), pltpu.VMEM((2, A_DV + V_PAD, tq), F32)]
    if ctx_queries:
        tq, nq, n_lat, out_rows = ctx, 1, 0, batch * ctx
        in_specs = head + [pl.BlockSpec((tq, LANES), lambda b, h, i: (ctx_blk + b, h))] + kv_ctx
        args = (lam_p, subln.reshape(A_DV, 1), qr, kr, vt)
        omap = lambda b, h, i: (b, h)
        scratch_shapes = scratch(tq)
    else:
        tq, out_rows = Q_TILE, nl
        nq, n_lat = seq // tq, seq // KV_TILE
        assert n_lat % 2 == 0
        omap = lambda b, h, i: (b * nq + i, h)
        in_specs = head + [pl.BlockSpec((tq, LANES), omap)] + kv_ctx + [
            pl.BlockSpec((seq, LANES), lambda b, h, i: (b, h)),
            pl.BlockSpec((1, n_lat, A_DV + V_PAD, KV_TILE), lambda b, h, i: (h, b, 0, 0))]
        args = (lam_p, subln.reshape(A_DV, 1), qr, kr, vt, kr, vt)
        scratch_shapes = scratch(tq) + [pltpu.VMEM((2, KV_TILE, tq), F32)] * 2
    return pl.pallas_call(
        functools.partial(_attn_kernel, n_lat=n_lat, lam_init=lam_init),
        grid=(batch, A_HEADS, nq),
        in_specs=in_specs,
        out_specs=pl.BlockSpec((tq, LANES), omap),
        out_shape=jax.ShapeDtypeStruct((out_rows, MIX_W), F32),
        scratch_shapes=scratch_shapes,
        compiler_params=pltpu.CompilerParams(dimension_semantics=("arbitrary", "arbitrary", "arbitrary")),
        name='diff_attn_ctx' if ctx_queries else 'diff_attn',
    )(*args)


def _merge_kernel(x_ref, mod_ref, hm_ref, mo_ref, on_ref, yr_ref, ya_ref, g_ref, wb_ref, wo_ref, o_ref):
    hm = hm_ref[0] + hm_ref[1]
    on = on_ref[...]
    parts = []
    for h in range(M_HEADS):
        blk = hm[:, h * M_DV:(h + 1) * M_DV]
        ms = jnp.mean(blk * blk, axis=-1, keepdims=True)
        parts.append(blk * lax.rsqrt(ms + NORM_EPS) * on[:, h * M_DV:(h + 1) * M_DV])
    ym = jnp.concatenate(parts, axis=1) * _sigmoid(mo_ref[...])
    d = x_ref.shape[1]
    zsum = _sigmoid(g_ref[:, 0:d]) * _dot(ym.astype(BF16), wb_ref[0])
    zsum += _sigmoid(g_ref[:, d:2 * d]) * _dot(yr_ref[...].astype(BF16), wb_ref[1])
    zsum += _sigmoid(g_ref[:, 2 * d:3 * d]) * _dot(ya_ref[...].astype(BF16), wb_ref[2])
    o_ref[...] = x_ref[...] + mod_ref[0, 5:6, :] * _dot(zsum.astype(BF16), wo_ref[...])


def _merge(xr, mods, hm, z, out_norm, yr, ya, w_branch, w_o, rows, stream_of_tile):
    d = xr.shape[1]
    tm = ROW_TILE
    full = lambda *s: pl.BlockSpec(s, lambda i: (0,) * len(s))
    return pl.pallas_call(
        _merge_kernel,
        grid=(rows // tm,),
        in_specs=[pl.BlockSpec((tm, d), lambda i: (i, 0)),
                  pl.BlockSpec((1, N_ADA, d), lambda i: (stream_of_tile(i), 0, 0)),
                  pl.BlockSpec((2, tm, MIX_W), lambda i: (0, i, 0)),
                  pl.BlockSpec((tm, MIX_W), lambda i: (i, Z_MO // MIX_W)),
                  full(1, MIX_W),
                  pl.BlockSpec((tm, MIX_W), lambda i: (i, 0)),
                  pl.BlockSpec((tm, MIX_W), lambda i: (i, 0)),
                  pl.BlockSpec((tm, 3 * d), lambda i: (i, Z_G // (3 * d))),
                  full(3, MIX_W, d), full(d, d)],
        out_specs=pl.BlockSpec((tm, d), lambda i: (i, 0)),
        out_shape=jax.ShapeDtypeStruct((rows, d), F32),
        compiler_params=pltpu.CompilerParams(dimension_semantics=("arbitrary",)),
        name='merge',
    )(xr, mods, hm, z, out_norm.reshape(1, MIX_W), yr, ya, z, w_branch, w_o)


def _lambda_init(layer):
    return 0.8 - 0.6 * math.exp(-0.3 * layer)


def kernel(x, c, ctx, c_ctx, w_ada, b_ada, norm_g, ffn1_w_gu, ffn1_w_down, ffn2_w_gu, ffn2_w_down,
           w_in, m_conv, m_gate_bias, m_out_norm, r_conv, r_w0, r_w2, r_a0, r_a2, r_g2, r_kk, r_ka,
           r_rk, r_ln_w, r_ln_b, a_qk_norm, a_lambda, a_subln, w_branch, w_o):
    batch, seq, d = x.shape
    ctx_len = ctx.shape[1]
    depth = w_ada.shape[0]
    nl, nc = batch * seq, batch * ctx_len
    rows = nl + nc
    geom = (nl, seq, ctx_len)
    tm = ROW_TILE
    assert seq % tm == 0 and nc % tm == 0 and KV_TILE == tm and KV_TILE % ctx_len == 0 and nl % ctx_len == 0
    assert ctx_len % CHUNK == 0 and seq % GRID_W == 0 and d == 1024

    tiles_per_seq = seq // tm
    stream_of_tile = lambda i: jnp.minimum(i // tiles_per_seq, batch)

    cv = jnp.zeros((SUBLANES, d), F32).at[:batch].set(c).at[batch].set(c_ctx)
    mods_all = _ada_all(cv, w_ada, b_ada)[:, :batch + 1].reshape(depth, batch + 1, N_ADA, d)

    zcols = _z_column_sources()
    zsrc = jnp.asarray(np.maximum(zcols, 0))
    zmask = jnp.asarray((zcols >= 0).astype(np.float32))
    cos_t, sin_t = _rope_tables(seq, tm)

    xr = jnp.concatenate([x.reshape(nl, d), ctx.reshape(nc, d)], axis=0)
    for li in range(depth):
        last = li == depth - 1
        mods = mods_all[li]
        w_z = (jnp.take(w_in[li], zsrc, axis=1) * zmask).astype(BF16)
        w_z = w_z.reshape(d, Z_W // Z_TILE, Z_TILE).transpose(1, 0, 2)

        xr = _ffn(xr, mods, norm_g[li, 0], ffn1_w_gu[li].astype(BF16), ffn1_w_down[li].astype(BF16),
                  0, rows, stream_of_tile)
        z = _inproj(xr, mods, norm_g[li, 1], w_z, stream_of_tile)

        qk_act, gcol = _mprep(z, m_conv[li], m_gate_bias[li], geom)
        qk_hm = qk_act.reshape(rows, 2 * M_HEADS, M_DQK).transpose(1, 0, 2)
        grow = gcol[:, :16].reshape(rows // CHUNK, CHUNK, 16).transpose(0, 2, 1)
        hm = _mlstm_scan(qk_hm, z, gcol, grow, geom, batch)

        rkv_c, g_r, vb, *scan_in = _rprep(z, r_conv[li], r_w0[li], r_w2[li], r_a0[li], r_a2[li], r_g2[li],
                                           r_kk[li], r_ka[li], geom)
        y_tm = _rwkv_scan(*scan_in[:6], vb, scan_in[6], geom, batch)
        yr = _rpost(y_tm, rkv_c, g_r, r_ln_w[li], r_ln_b[li], r_rk[li])

        qr, kr, vt = _aprep(z, cos_t, sin_t, a_qk_norm[li], geom)
        att = lambda cq: _attention(qr, kr, vt, a_lambda[li], a_subln[li], _lambda_init(li), geom, batch, cq)
        ya = att(False) if last else jnp.concatenate([att(False), att(True)], axis=0)

        out_rows = nl if last else rows
        xr = _merge(xr, mods, hm, z, m_out_norm[li], yr, ya, w_branch[li].astype(BF16),
                    w_o[li].astype(BF16), out_rows, stream_of_tile)
        xr = _ffn(xr, mods, norm_g[li, 2], ffn2_w_gu[li].astype(BF16), ffn2_w_down[li].astype(BF16),
                  6, out_rows, stream_of_tile)
    return xr[:nl].reshape(batch, seq, d)
```

```python
import functools
import math

import numpy as np
import jax
import jax.numpy as jnp
from jax import lax
from jax.experimental import pallas as pl
from jax.experimental.pallas import tpu as pltpu

F32 = jnp.float32
BF16 = jnp.bfloat16
HI = lax.Precision.HIGHEST

N_ADA = 9
MACARON_W = 0.5
NORM_EPS = 1e-6
MIX_W = 512
GRID_W = 64
CHUNK = 64
SCAN_CHUNKS = 4
M_SCAN_CHUNKS = 2

M_HEADS = 4
M_DQK = 64
M_DV = 128
R_HEADS = 8
R_DH = 64
R_GN_EPS = 64e-5
A_HEADS = 4
A_DH = 64
A_DV = 128
ROPE_BASE = 10000.0

LANES = 128
SUBLANES = 8
HALO_ROWS = 16
ROW_TILE = 512
RWKV_PREP_TILE = 256
Q_TILE = 512
KV_TILE = 512
V_PAD = 16
LOG2_E = math.log2(math.e)
ATT_BUFS = 4
ATT_LOOKBACK = 2
FFN_CHUNK = 256
FFN_VMEM_LIMIT = 48 * 2 ** 20

_IN_SPLITS = (
    ('m_q', 256), ('m_k', 256), ('m_v', 512), ('m_o', 512),
    ('m_if', 4), ('m_ff', 4), ('m_ib', 4), ('m_fb', 4),
    ('r_r', 512), ('r_k', 512), ('r_v', 512),
    ('r_wf', 64), ('r_wb', 64), ('r_af', 64), ('r_ab', 64), ('r_g', 128),
    ('a_q', 512), ('a_k', 512), ('a_v', 512),
    ('g_m', 1024), ('g_r', 1024), ('g_a', 1024),
)
Z_G, Z_MQK, Z_MV, Z_MO, Z_RKV, Z_AQ, Z_AK, Z_AV, Z_LORA, Z_MG, Z_W = (
    0, 3072, 3584, 4096, 4608, 6144, 6656, 7168, 7680, 8064, 8192)
Z_TILE = 1024


def _deinterleave64():
    return np.concatenate([np.arange(0, 64, 2), np.arange(1, 64, 2)])


def _z_column_sources():
    off, start = {}, 0
    for name, w in _IN_SPLITS:
        off[name] = start
        start += w
    cols = []
    rng = lambda n: list(range(off[n], off[n] + dict(_IN_SPLITS)[n]))
    cols += rng('g_m') + rng('g_r') + rng('g_a')
    cols += rng('m_q') + rng('m_k') + rng('m_v') + rng('m_o')
    cols += rng('r_r') + rng('r_k') + rng('r_v')
    perm = _deinterleave64()
    for n in ('a_q', 'a_k'):
        for g in range(8):
            cols += list(off[n] + g * 64 + perm)
    cols += rng('a_v')
    cols += rng('r_wf') + rng('r_wb') + rng('r_af') + rng('r_ab') + rng('r_g')
    cols += rng('m_if') + rng('m_ff') + rng('m_ib') + rng('m_fb')
    cols += [-1] * (Z_W - len(cols))
    assert len(cols) == Z_W
    return np.asarray(cols, np.int32)


def _sigmoid(x):
    return 1.0 / (1.0 + jnp.exp(-x))


def _dot(a, b, prec=None):
    return jnp.dot(a, b, preferred_element_type=F32, precision=prec)


def _dot_nt(a, b, prec=None):
    return lax.dot_general(a, b, (((1,), (1,)), ((), ())), preferred_element_type=F32, precision=prec)


def _dot_tn(a, b, prec=None):
    return lax.dot_general(a, b, (((0,), (0,)), ((), ())), preferred_element_type=F32, precision=prec)


def _norm_mod(x, g, shift, scale):
    ms = jnp.mean(x * x, axis=-1, keepdims=True)
    y = x * lax.rsqrt(ms + NORM_EPS) * g
    return y * (1.0 + scale) + shift


def _ada_kernel(c_ref, w_ref, b_ref, o_ref):
    c = c_ref[...]
    s = c * _sigmoid(c)
    o_ref[0] = _dot(s, w_ref[0], HI) + b_ref[0]


def _ada_all(cv, w_ada, b_ada):
    depth, d, nd = w_ada.shape
    tn = 1024
    return pl.pallas_call(
        _ada_kernel,
        grid=(depth, nd // tn),
        in_specs=[pl.BlockSpec((SUBLANES, d), lambda l, n: (0, 0)),
                  pl.BlockSpec((1, d, tn), lambda l, n: (l, 0, n)),
                  pl.BlockSpec((1, 1, tn), lambda l, n: (l, 0, n))],
        out_specs=pl.BlockSpec((1, SUBLANES, tn), lambda l, n: (l, 0, n)),
        out_shape=jax.ShapeDtypeStruct((depth, SUBLANES, nd), F32),
        name='ada',
    )(cv, w_ada, b_ada.reshape(depth, 1, nd))


def _ffn_kernel(x_ref, mod_ref, g_ref, wgu_ref, wd_ref, o_ref, u_scr, *, mi):
    dff = wd_ref.shape[0]
    h = _norm_mod(x_ref[...], g_ref[...], mod_ref[0, mi:mi + 1, :], mod_ref[0, mi + 1:mi + 2, :]).astype(BF16)
    for j in range(dff // FFN_CHUNK):
        lo, hi = j * FFN_CHUNK, (j + 1) * FFN_CHUNK
        a = _dot(h, wgu_ref[:, lo:hi])
        b = _dot(h, wgu_ref[:, dff + lo:dff + hi])
        u_scr[:, lo:hi] = ((a * _sigmoid(a)) * b).astype(BF16)
    o_ref[...] = x_ref[...] + MACARON_W * mod_ref[0, mi + 2:mi + 3, :] * _dot(u_scr[...], wd_ref[...])


def _ffn(xr, mods, g, w_gu, w_down, mi, rows, stream_of_tile):
    d = xr.shape[1]
    dff = w_down.shape[0]
    assert dff % FFN_CHUNK == 0
    tm = ROW_TILE
    resident = lambda *s: pl.BlockSpec(s, lambda i: (0,) * len(s), pipeline_mode=pl.Buffered(1))
    return pl.pallas_call(
        functools.partial(_ffn_kernel, mi=mi),
        grid=(rows // tm,),
        in_specs=[pl.BlockSpec((tm, d), lambda i: (i, 0)),
                  pl.BlockSpec((1, N_ADA, d), lambda i: (stream_of_tile(i), 0, 0)),
                  pl.BlockSpec((1, d), lambda i: (0, 0)),
                  resident(d, 2 * dff), resident(dff, d)],
        out_specs=pl.BlockSpec((tm, d), lambda i: (i, 0)),
        out_shape=jax.ShapeDtypeStruct((rows, d), F32),
        scratch_shapes=[pltpu.VMEM((tm, dff), BF16)],
        compiler_params=pltpu.CompilerParams(dimension_semantics=("arbitrary",),
                                             vmem_limit_bytes=FFN_VMEM_LIMIT),
        name='ffn',
    )(xr, mods, g.reshape(1, d), w_gu, w_down)


def _inproj_kernel(x_ref, mod_ref, g_ref, w_ref, o_ref, h_scr):
    @pl.when(pl.program_id(1) == 0)
    def _():
        h = _norm_mod(x_ref[...], g_ref[...], mod_ref[0, 3:4, :], mod_ref[0, 4:5, :])
        h_scr[...] = h.astype(BF16)

    o_ref[...] = _dot(h_scr[...], w_ref[pl.program_id(1)]).astype(o_ref.dtype)


def _inproj(xr, mods, g, w_z, stream_of_tile):
    rows, d = xr.shape
    tm = ROW_TILE
    nz = Z_W // Z_TILE
    return pl.pallas_call(
        _inproj_kernel,
        grid=(rows // tm, nz),
        in_specs=[pl.BlockSpec((tm, d), lambda i, n: (i, 0)),
                  pl.BlockSpec((1, N_ADA, d), lambda i, n: (stream_of_tile(i), 0, 0)),
                  pl.BlockSpec((1, d), lambda i, n: (0, 0)),
                  pl.BlockSpec((nz, d, Z_TILE), lambda i, n: (0, 0, 0), pipeline_mode=pl.Buffered(1))],
        out_specs=pl.BlockSpec((tm, Z_TILE), lambda i, n: (i, n)),
        out_shape=jax.ShapeDtypeStruct((rows, Z_W), BF16),
        scratch_shapes=[pltpu.VMEM((tm, d), BF16)],
        compiler_params=pltpu.CompilerParams(dimension_semantics=("arbitrary", "arbitrary"),
                                             vmem_limit_bytes=FFN_VMEM_LIMIT),
        name='inproj',
    )(xr, mods, g.reshape(1, d), w_z)


def _conv3(x, prev_rows, next_rows, w, row0, geom):
    nl, seq, ctx = geom
    tm = x.shape[0]
    x = x.astype(F32)
    t = lax.broadcasted_iota(jnp.int32, (tm, 1), 0)
    r = row0 + t
    is_lat = r < nl
    pos = jnp.where(is_lat, lax.rem(r, seq), lax.rem(jnp.maximum(r - nl, 0), ctx))
    seglen = jnp.where(is_lat, seq, ctx)
    xm = pltpu.roll(x, 1, 0)
    xm = jnp.where(t == 0, prev_rows[HALO_ROWS - 1:HALO_ROWS, :].astype(F32), xm)
    xm = jnp.where(pos == 0, 0.0, xm)
    xp = pltpu.roll(x, tm - 1, 0)
    xp = jnp.where(t == tm - 1, next_rows[0:1, :].astype(F32), xp)
    xp = jnp.where(pos == seglen - 1, 0.0, xp)
    return xm * w[0:1, :] + x * w[1:2, :] + xp * w[2:3, :]


def _halo_specs(tm, width, col_blk, rows):
    per = tm // HALO_ROWS
    last = rows // HALO_ROWS - 1
    return [pl.BlockSpec((tm, width), lambda i: (i, col_blk)),
            pl.BlockSpec((HALO_ROWS, width), lambda i: (jnp.maximum(i * per - 1, 0), col_blk)),
            pl.BlockSpec((HALO_ROWS, width), lambda i: (jnp.minimum((i + 1) * per, last), col_blk))]


def _mprep_kernel(x_ref, xp_ref, xn_ref, w_ref, gt_ref, gb_ref, qk_ref, go_ref, *, geom):
    tm = x_ref.shape[0]
    y = _conv3(x_ref[...], xp_ref[...], xn_ref[...], w_ref[...], pl.program_id(0) * tm, geom)
    y = y * _sigmoid(y)
    lane = lax.broadcasted_iota(jnp.int32, (1, y.shape[1]), 1)
    qk_ref[...] = jnp.where(lane >= M_HEADS * M_DQK, y * (M_DQK ** -0.5), y)
    g = gt_ref[...].astype(F32) + gb_ref[...]
    gl = lax.broadcasted_iota(jnp.int32, (1, LANES), 1)
    is_forget = (lax.rem(gl, 2 * M_HEADS) >= M_HEADS) & (gl < 4 * M_HEADS)
    logsig = jnp.minimum(g, 0.0) - jnp.log(1.0 + jnp.exp(-jnp.abs(g)))
    go_ref[...] = jnp.where(is_forget, logsig, g)


def _mprep(z, conv_w, gate_bias, geom):
    rows = z.shape[0]
    tm = ROW_TILE
    gb = jnp.zeros((1, LANES), F32).at[0, :4 * M_HEADS].set(gate_bias.reshape(-1))
    return pl.pallas_call(
        functools.partial(_mprep_kernel, geom=geom),
        grid=(rows // tm,),
        in_specs=_halo_specs(tm, 512, Z_MQK // 512, rows) + [
            pl.BlockSpec((3, 512), lambda i: (0, 0)),
            pl.BlockSpec((tm, LANES), lambda i: (i, Z_MG // LANES)),
            pl.BlockSpec((1, LANES), lambda i: (0, 0))],
        out_specs=[pl.BlockSpec((tm, 512), lambda i: (i, 0)),
                   pl.BlockSpec((tm, LANES), lambda i: (i, 0))],
        out_shape=[jax.ShapeDtypeStruct((rows, 512), F32), jax.ShapeDtypeStruct((rows, LANES), F32)],
        compiler_params=pltpu.CompilerParams(dimension_semantics=("arbitrary",)),
        name='mlstm_prep',
    )(z, z, z, conv_w, z, gb)


def _chunk_block(d, b, j, geom, size=CHUNK):
    nl, seq, ctx = geom
    nc_c, nc_l = ctx // size, seq // size
    jl = j - nc_c
    c_ctx = jnp.where(d == 0, j, nc_c - 1 - j)
    c_lat = jnp.where(d == 0, jl, nc_l - 1 - jl)
    return jnp.where(j < nc_c, nl // size + b * nc_c + c_ctx, b * nc_l + c_lat)


def _order_masks(d):
    ti = lax.broadcasted_iota(jnp.int32, (CHUNK, CHUNK), 0)
    si = lax.broadcasted_iota(jnp.int32, (CHUNK, CHUNK), 1)
    fwd = d == 0
    ahead = jnp.where(fwd, si - ti, ti - si)
    incl = ahead <= 0
    strict = ahead < 0
    incl_t = ahead >= 0
    return fwd, incl, strict, incl_t, ti == si


def _mlstm_kernel(qk_ref, v_ref, gc_ref, gr_ref, o_ref, ct_scr, m_scr):
    d = pl.program_id(0)

    @pl.when(pl.program_id(2) == 0)
    def _():
        ct_scr[...] = jnp.zeros_like(ct_scr)
        m_scr[...] = jnp.zeros_like(m_scr)

    fwd, incl, _, incl_t, _ = _order_masks(d)
    heads = range(M_HEADS)
    chunks = range(M_SCAN_CHUNKS)
    lane = lax.broadcasted_iota(jnp.int32, (CHUNK, LANES), 1)
    ones_col = jnp.where(lane == 0, 1.0, 0.0)
    cat1 = lambda xs: jnp.concatenate(xs, axis=1)

    pre = []
    for u in chunks:
        cu = jnp.where(fwd, u, M_SCAN_CHUNKS - 1 - u)
        rows = pl.ds(pl.multiple_of(cu * CHUNK, CHUNK), CHUNK)
        gc = gc_ref[rows, :]
        gr = gr_ref[cu]
        q = [qk_ref[h, rows, :].astype(BF16) for h in heads]
        k = [qk_ref[M_HEADS + h, rows, :].astype(BF16) for h in heads]
        v = [v_ref[rows, h * M_DV:(h + 1) * M_DV].astype(F32) for h in heads]
        qk = [_dot_nt(q[h], k[h]) for h in heads]
        per_head = []
        for h in heads:
            i_col = jnp.where(fwd, gc[:, h:h + 1], gc[:, 8 + h:9 + h])
            f_col = jnp.where(fwd, gc[:, 4 + h:5 + h], gc[:, 12 + h:13 + h])
            i_row = jnp.where(fwd, gr[h:h + 1, :], gr[8 + h:9 + h, :])
            f_row = jnp.where(fwd, gr[4 + h:5 + h, :], gr[12 + h:13 + h, :])
            bcum_col = jnp.sum(jnp.where(incl, f_row, 0.0), axis=1, keepdims=True)
            bcum_row = jnp.sum(jnp.where(incl_t, f_col, 0.0), axis=0, keepdims=True)
            dlog = jnp.where(incl, bcum_col - bcum_row + i_row, -jnp.inf)
            rmax = jnp.max(dlog, axis=1, keepdims=True)
            btot = jnp.sum(f_col, axis=0, keepdims=True)
            wlog = btot - bcum_col + i_col
            per_head.append(dict(bcum=bcum_col, rmax=rmax, btot=btot, wlog=wlog,
                                 wmax=jnp.max(wlog, axis=0, keepdims=True),
                                 qkd=(qk[h] * jnp.exp(dlog - rmax)).astype(BF16)))
        v_aug = [cat1([v[h], ones_col]) for h in heads]
        sv0 = [_dot(per_head[h]['qkd'], v_aug[h].astype(BF16)) for h in heads]
        pre.append(dict(rows=rows, q=q, k=k, v_aug=v_aug, sv0=sv0, g=per_head))

    ct = [ct_scr[h] for h in heads]
    m_st = [m_scr[h] for h in heads]
    for u in chunks:
        c = pre[u]
        m_new = [jnp.maximum(c['g'][h]['btot'] + m_st[h], c['g'][h]['wmax']) for h in heads]
        ws = [jnp.exp(c['g'][h]['wlog'] - m_new[h]) for h in heads]
        kv = [_dot_tn(c['k'][h], (ws[h] * c['v_aug'][h]).astype(BF16)) for h in heads]
        qc = [_dot(c['q'][h], ct[h].astype(BF16)) for h in heads]
        for h in heads:
            g = c['g'][h]
            inter = g['bcum'] + m_st[h]
            mt = jnp.maximum(inter, g['rmax'])
            nd = jnp.exp(g['rmax'] - mt) * c['sv0'][h] + jnp.exp(inter - mt) * qc[h]
            den = jnp.maximum(jnp.abs(nd[:, M_DV:M_DV + 1]), jnp.exp(-mt))
            o_ref[0, c['rows'], h * M_DV:(h + 1) * M_DV] = nd[:, :M_DV] / den
            ct[h] = jnp.exp(g['btot'] + m_st[h] - m_new[h]) * ct[h] + kv[h]
            m_st[h] = m_new[h]
    for h in heads:
        ct_scr[h] = ct[h]
        m_scr[h] = m_st[h]


def _mlstm_scan(qk_hm, z, gcol, grow, geom, batch):
    rows = z.shape[0]
    step = M_SCAN_CHUNKS * CHUNK
    assert geom[1] % step == 0 and geom[2] % step == 0
    nch = (geom[1] + geom[2]) // step
    blk = lambda d, b, j: _chunk_block(d, b, j, geom, step)
    return pl.pallas_call(
        _mlstm_kernel,
        grid=(2, batch, nch),
        in_specs=[pl.BlockSpec((2 * M_HEADS, step, M_DQK), lambda d, b, j: (0, blk(d, b, j), 0)),
                  pl.BlockSpec((step, MIX_W), lambda d, b, j: (blk(d, b, j), Z_MV // MIX_W)),
                  pl.BlockSpec((step, LANES), lambda d, b, j: (blk(d, b, j), 0)),
                  pl.BlockSpec((M_SCAN_CHUNKS, 16, CHUNK), lambda d, b, j: (blk(d, b, j), 0, 0))],
        out_specs=pl.BlockSpec((1, step, MIX_W), lambda d, b, j: (d, blk(d, b, j), 0)),
        out_shape=jax.ShapeDtypeStruct((2, rows, MIX_W), F32),
        scratch_shapes=[pltpu.VMEM((M_HEADS, M_DQK, M_DV + LANES), F32),
                        pltpu.VMEM((M_HEADS, 1, 1), F32)],
        compiler_params=pltpu.CompilerParams(dimension_semantics=("arbitrary", "arbitrary", "arbitrary")),
        name='mlstm_scan',
    )(qk_hm, z, gcol, grow)


def _rprep_kernel(x_ref, xp_ref, xn_ref, cw_ref, lo_ref, w0_ref, w2_ref, a0_ref, a2_ref, g2_ref,
                  kkw_ref, kaw_ref, bd_ref, tri_ref,
                  rkv_ref, g_ref, vb_ref, at_ref, bt_ref, kt_ref, rt_ref, bb_ref, kb_ref, et_ref, *, geom):
    tm = x_ref.shape[0]
    rkv = _conv3(x_ref[...], xp_ref[...], xn_ref[...], cw_ref[...], pl.program_id(0) * tm, geom)
    rkv_ref[...] = rkv
    r = rkv[:, 0:MIX_W]
    k = rkv[:, MIX_W:2 * MIX_W]
    vb_ref[...] = rkv[:, 2 * MIX_W:3 * MIX_W].astype(BF16)
    lo = lo_ref[...].astype(F32)
    g_ref[...] = _dot(_sigmoid(lo[:, 256:384]), g2_ref[...], HI)
    kkr = k * kkw_ref[...]
    kk = kkr / jnp.maximum(jnp.sqrt(_group_sum(kkr * kkr, bd_ref[...])), 1e-12)
    for d in range(2):
        w_raw = w0_ref[d:d + 1, :] + _dot(jnp.tanh(lo[:, 64 * d:64 * d + 64]), w2_ref[d], HI)
        lw = -_sigmoid(w_raw) * math.exp(-0.5)
        a = _sigmoid(a0_ref[d:d + 1, :] + _dot(lo[:, 128 + 64 * d:192 + 64 * d], a2_ref[d], HI))
        hi = lw.astype(BF16)
        rem = lw - hi.astype(F32)
        mid = rem.astype(BF16)
        low = (rem - mid.astype(F32)).astype(BF16)
        parts = jnp.concatenate([hi, mid, low], axis=1)
        c3 = _dot(tri_ref[d, 0], parts)
        s3 = _dot(tri_ref[d, 1], parts)
        cum = c3[:, 0:MIX_W] + c3[:, MIX_W:2 * MIX_W] + c3[:, 2 * MIX_W:3 * MIX_W]
        suf = s3[:, 0:MIX_W] + s3[:, MIX_W:2 * MIX_W] + s3[:, 2 * MIX_W:3 * MIX_W]
        kd = k * (1.0 + (a - 1.0) * kaw_ref[...])
        kka = kk * a
        e_neg = jnp.exp(-cum)
        e_end = jnp.exp(suf)
        at_ref[d] = (-kk * jnp.exp(cum - lw)).astype(BF16)
        bt_ref[d] = (kka * e_neg).astype(BF16)
        kt_ref[d] = (kd * e_neg).astype(BF16)
        rt_ref[d] = (r * jnp.exp(cum)).astype(BF16)
        bb_ref[d] = (kka * e_end).astype(BF16)
        kb_ref[d] = (kd * e_end).astype(BF16)
        et_ref[d] = jnp.exp(cum + suf)


def _chunk_order_matrices(tm):
    i = np.arange(tm)
    same = (i[:, None] // CHUNK) == (i[None, :] // CHUNK)
    le = i[None, :] <= i[:, None]
    ge = i[None, :] >= i[:, None]
    mats = np.stack([np.stack([same & le, same & ~le]), np.stack([same & ge, same & ~ge])])
    return jnp.asarray(mats.astype(np.float32), dtype=BF16)


def _rprep(z, conv_w, w0, w2, a0, a2, g2, kk_w, ka_w, geom):
    rows = z.shape[0]
    tm = RWKV_PREP_TILE
    full = lambda *s: pl.BlockSpec(s, lambda i: (0,) * len(s))
    row = lambda w: pl.BlockSpec((tm, w), lambda i: (i, 0))
    both = pl.BlockSpec((2, tm, MIX_W), lambda i: (0, i, 0))
    shp = lambda dt: jax.ShapeDtypeStruct((2, rows, MIX_W), dt)
    return pl.pallas_call(
        functools.partial(_rprep_kernel, geom=geom),
        grid=(rows // tm,),
        in_specs=_halo_specs(tm, 1536, Z_RKV // 1536, rows) + [
            full(3, 1536),
            pl.BlockSpec((tm, 384), lambda i: (i, Z_LORA // 384)),
            full(2, 512), full(2, 64, 512), full(2, 512), full(2, 64, 512), full(128, 512),
            full(1, MIX_W), full(1, MIX_W), full(LANES, LANES), full(2, 2, tm, tm)],
        out_specs=[row(1536), row(MIX_W), row(MIX_W)] + [both] * 7,
        out_shape=[jax.ShapeDtypeStruct((rows, 1536), F32), jax.ShapeDtypeStruct((rows, MIX_W), F32),
                   jax.ShapeDtypeStruct((rows, MIX_W), BF16)] + [shp(BF16)] * 6 + [shp(F32)],
        compiler_params=pltpu.CompilerParams(dimension_semantics=("arbitrary",)),
        name='rwkv_prep',
    )(z, z, z, conv_w, z, w0, w2, a0, a2, g2, kk_w.reshape(1, MIX_W), ka_w.reshape(1, MIX_W),
      _group_ones(R_DH), _chunk_order_matrices(tm))


def _rwkv_kernel(at_ref, bt_ref, kt_ref, rt_ref, bb_ref, kb_ref, v_ref, et_ref, y_ref, ss_scr):
    d = pl.program_id(0)

    @pl.when(pl.program_id(2) == 0)
    def _():
        ss_scr[...] = jnp.zeros_like(ss_scr)

    ti = lax.broadcasted_iota(jnp.int32, (CHUNK, LANES), 0)
    li = lax.broadcasted_iota(jnp.int32, (CHUNK, LANES), 1)
    si = jnp.bitwise_and(li, R_DH - 1)
    ahead = jnp.where(d == 0, si - ti, ti - si)
    strict2 = ahead < 0
    incl2 = ahead <= 0
    first = li < R_DH
    eye64 = (lax.broadcasted_iota(jnp.int32, (CHUNK, R_DH), 0)
             == lax.broadcasted_iota(jnp.int32, (CHUNK, R_DH), 1)).astype(F32)
    rr = lax.broadcasted_iota(jnp.int32, (LANES, LANES), 0)
    cc = lax.broadcasted_iota(jnp.int32, (LANES, LANES), 1)
    same_head = (rr < R_DH) == (cc < R_DH)
    eye128 = rr == cc
    zero = jnp.zeros((CHUNK, LANES), BF16)
    cat0 = lambda xs: jnp.concatenate(xs, axis=0)
    cat1 = lambda xs: jnp.concatenate(xs, axis=1)
    keep = lambda h, x: jnp.where(first, x, zero) if h == 0 else jnp.where(first, zero, x)

    pairs = range(R_HEADS // 2)
    items = [(u, p) for u in range(SCAN_CHUNKS) for p in pairs]
    rows_of = [pl.ds(pl.multiple_of(jnp.where(d == 0, u, SCAN_CHUNKS - 1 - u) * CHUNK, CHUNK), CHUNK)
               for u in range(SCAN_CHUNKS)]
    lanes_of = [slice(p * LANES, (p + 1) * LANES) for p in pairs]
    at = [at_ref[0, rows_of[u], lanes_of[p]] for u, p in items]
    rt = [rt_ref[0, rows_of[u], lanes_of[p]] for u, p in items]
    v = [v_ref[rows_of[u], lanes_of[p]] for u, p in items]
    n = range(len(items))
    g = [_dot_nt(cat0([keep(0, at[i]), keep(0, rt[i]), keep(1, at[i]), keep(1, rt[i])]),
                 cat0([bt_ref[0, rows_of[u], lanes_of[p]], kt_ref[0, rows_of[u], lanes_of[p]]]))
         for i, (u, p) in enumerate(items)]
    ga = [[jnp.where(strict2, g[i][2 * h * CHUNK:(2 * h + 1) * CHUNK], 0.0) for h in range(2)] for i in n]
    gr = [[jnp.where(incl2, g[i][(2 * h + 1) * CHUNK:(2 * h + 2) * CHUNK], 0.0) for h in range(2)] for i in n]
    aakv = [_dot(cat1(ga[i]).astype(BF16), cat0([zero, keep(0, v[i]), zero, keep(1, v[i])])).astype(BF16)
            for i in n]
    pw = [ga[i][h][:, :R_DH] for i in n for h in range(2)]
    tinv = [eye64 + a for a in pw]
    for _ in range(int(math.log2(CHUNK)) - 1):
        pwb = [a.astype(BF16) for a in pw]
        pw = [_dot(a, a) for a in pwb]
        tinv = [t + _dot(t.astype(BF16), a.astype(BF16)) for t, a in zip(tinv, pw)]
    tinv = [t.astype(BF16) for t in tinv]
    x = [(_dot(tinv[2 * i], cat1([keep(0, at[i]), keep(0, aakv[i])]))
          + _dot(tinv[2 * i + 1], cat1([keep(1, at[i]), keep(1, aakv[i])]))).astype(BF16) for i in n]
    y4 = [_dot(cat1(gr[i]).astype(BF16),
               cat0([cat1([keep(0, x[i][:, :LANES]), keep(0, x[i][:, LANES:])]), cat1([zero, keep(0, v[i])]),
                     cat1([keep(1, x[i][:, :LANES]), keep(1, x[i][:, LANES:])]), cat1([zero, keep(1, v[i])])]))
          for i in n]
    mn = [_dot_tn(cat0([bb_ref[0, rows_of[u], lanes_of[p]], kb_ref[0, rows_of[u], lanes_of[p]]]),
                  cat0([x[i], cat1([zero, v[i]])])) for i, (u, p) in enumerate(items)]
    lhs = []
    for i, (u, p) in enumerate(items):
        rh = rt[i].astype(F32) + y4[i][:, :LANES]
        decay = et_ref[0, pl.ds(jnp.where(d == 0, u, SCAN_CHUNKS - 1 - u) * CHUNK, 1), lanes_of[p]]
        mt = jnp.where(eye128, decay, 0.0) + jnp.where(same_head, mn[i][:, :LANES], 0.0)
        lhs.append(cat0([rh, mt]).astype(BF16))
    ss = [ss_scr[p] for p in pairs]
    for u in range(SCAN_CHUNKS):
        out = [_dot(lhs[u * len(pairs) + p], ss[p].astype(BF16)) for p in pairs]
        for p in pairs:
            i = u * len(pairs) + p
            y_ref[0, rows_of[u], lanes_of[p]] = out[p][:CHUNK] + y4[i][:, LANES:]
            ss[p] = out[p][CHUNK:] + jnp.where(same_head, mn[i][:, LANES:], 0.0)
    for p in pairs:
        ss_scr[p] = ss[p]


def _rwkv_scan(at, bt, kt, rt, bb, kb, vb, et, geom, batch):
    rows = vb.shape[0]
    step = SCAN_CHUNKS * CHUNK
    assert geom[1] % step == 0 and geom[2] % step == 0
    nch = (geom[1] + geom[2]) // step
    blk = lambda d, b, j: _chunk_block(d, b, j, geom, step)
    per_dir = pl.BlockSpec((1, step, MIX_W), lambda d, b, j: (d, blk(d, b, j), 0))
    return pl.pallas_call(
        _rwkv_kernel,
        grid=(2, batch, nch),
        in_specs=[per_dir] * 6 + [pl.BlockSpec((step, MIX_W), lambda d, b, j: (blk(d, b, j), 0)), per_dir],
        out_specs=per_dir,
        out_shape=jax.ShapeDtypeStruct((2, rows, MIX_W), F32),
        scratch_shapes=[pltpu.VMEM((R_HEADS // 2, LANES, LANES), F32)],
        compiler_params=pltpu.CompilerParams(dimension_semantics=("arbitrary", "arbitrary", "arbitrary")),
        name='rwkv_scan',
    )(at, bt, kt, rt, bb, kb, vb, et)


def _rpost_kernel(y_ref, rkv_ref, g_ref, lnw_ref, lnb_ref, rk_ref, bd_ref, o_ref):
    y = y_ref[0] + y_ref[1]
    bd = bd_ref[...]
    inv = 1.0 / R_DH
    mu = _group_sum(y, bd) * inv
    yc = y - mu
    var = _group_sum(yc * yc, bd) * inv
    yn = yc * lax.rsqrt(var + R_GN_EPS) * lnw_ref[...] + lnb_ref[...]
    r = rkv_ref[:, 0:512]
    k = rkv_ref[:, 512:1024]
    v = rkv_ref[:, 1024:1536]
    bonus = _group_sum(r * k * rk_ref[...], bd) * v
    o_ref[...] = (yn + bonus) * g_ref[...]


def _group_ones(group):
    i = np.arange(LANES) // group
    return jnp.asarray((i[:, None] == i[None, :]).astype(np.float32), dtype=BF16)


def _group_sum(x, ones):
    tm, w = x.shape
    nt = w // LANES
    xs = jnp.concatenate([x[:, t * LANES:(t + 1) * LANES] for t in range(nt)], axis=0)
    hi = xs.astype(BF16)
    rem = xs - hi.astype(F32)
    mid = rem.astype(BF16)
    low = (rem - mid.astype(F32)).astype(BF16)
    y = _dot(jnp.concatenate([hi, mid, low], axis=0), ones)
    n = nt * tm
    ys = y[0:n] + y[n:2 * n] + y[2 * n:3 * n]
    return jnp.concatenate([ys[t * tm:(t + 1) * tm] for t in range(nt)], axis=1)


def _rpost(y_tm, rkv_c, g, ln_w, ln_b, r_k):
    rows = rkv_c.shape[0]
    tm = ROW_TILE
    full = lambda *s: pl.BlockSpec(s, lambda i: (0,) * len(s))
    return pl.pallas_call(
        _rpost_kernel,
        grid=(rows // tm,),
        in_specs=[pl.BlockSpec((2, tm, 512), lambda i: (0, i, 0)),
                  pl.BlockSpec((tm, 1536), lambda i: (i, 0)),
                  pl.BlockSpec((tm, 512), lambda i: (i, 0)),
                  full(1, 512), full(1, 512), full(1, 512), full(LANES, LANES)],
        out_specs=pl.BlockSpec((tm, 512), lambda i: (i, 0)),
        out_shape=jax.ShapeDtypeStruct((rows, 512), F32),
        compiler_params=pltpu.CompilerParams(dimension_semantics=("arbitrary",)),
        name='rwkv_post',
    )(y_tm, rkv_c, g, ln_w.reshape(1, 512), ln_b.reshape(1, 512), r_k.reshape(1, 512),
      _group_ones(R_DH))


def _aprep_kernel(q_ref, k_ref, v_ref, cos_ref, sin_ref, qn_ref, kn_ref, bd_ref, qo_ref, ko_ref, vt_ref):
    tm = v_ref.shape[0]
    pad_row = lax.broadcasted_iota(jnp.int32, (V_PAD, tm), 0)
    pad = jnp.where(pad_row == 0, 1.0, 0.0).astype(BF16)
    for h in range(A_HEADS):
        vt_ref[h, 0, 0:A_DV, :] = v_ref[:, h * A_DV:(h + 1) * A_DV].astype(F32).T.astype(BF16)
        vt_ref[h, 0, A_DV:A_DV + V_PAD, :] = pad
    bd = bd_ref[...]
    cos = cos_ref[...]
    sin = sin_ref[...]
    lane = lax.broadcasted_iota(jnp.int32, (1, 512), 1)
    lower = lax.rem(lane, A_DH) < A_DH // 2

    def one(x, g):
        ms = _group_sum(x * x, bd) * (1.0 / A_DH)
        y = x * lax.rsqrt(ms + NORM_EPS) * g
        swapped = jnp.where(lower, pltpu.roll(y, 512 - A_DH // 2, 1), pltpu.roll(y, A_DH // 2, 1))
        return y * cos + swapped * sin

    qo_ref[...] = one(q_ref[...].astype(F32), qn_ref[...]).astype(BF16)
    ko_ref[...] = one(k_ref[...].astype(F32), kn_ref[...]).astype(BF16)


def _rope_tables(seq, tm):
    rows = seq // GRID_W
    row = jnp.repeat(jnp.arange(rows, dtype=F32), GRID_W)
    col = jnp.tile(jnp.arange(GRID_W, dtype=F32), rows)
    half = A_DH // 2
    inv = ROPE_BASE ** (-jnp.arange(0, half, 2, dtype=F32) / half)
    ang = jnp.concatenate([row[:, None] * inv, col[:, None] * inv], axis=-1)
    cos, sin = jnp.cos(ang), jnp.sin(ang)
    cos64 = jnp.concatenate([cos, cos], axis=-1)
    sin64 = jnp.concatenate([-sin, sin], axis=-1)
    cos_t = jnp.concatenate([jnp.tile(cos64, (1, 8)), jnp.ones((tm, 512), F32)], axis=0)
    sin_t = jnp.concatenate([jnp.tile(sin64, (1, 8)), jnp.zeros((tm, 512), F32)], axis=0)
    return cos_t, sin_t


def _aprep(z, cos_t, sin_t, qk_norm, geom):
    rows = z.shape[0]
    nl, seq, _ = geom
    tm = ROW_TILE
    perm = _deinterleave64()
    qn = jnp.tile(qk_norm[0][perm] * (A_DH ** -0.5 * LOG2_E), 8).reshape(1, 512)
    kn = jnp.tile(qk_norm[1][perm], 8).reshape(1, 512)
    tiles_per_seq = seq // tm
    tab = lambda i: (jnp.where(i < nl // tm, lax.rem(i, tiles_per_seq), tiles_per_seq), 0)
    full = lambda *s: pl.BlockSpec(s, lambda i: (0,) * len(s))
    return pl.pallas_call(
        _aprep_kernel,
        grid=(rows // tm,),
        in_specs=[pl.BlockSpec((tm, 512), lambda i: (i, Z_AQ // 512)),
                  pl.BlockSpec((tm, 512), lambda i: (i, Z_AK // 512)),
                  pl.BlockSpec((tm, 512), lambda i: (i, Z_AV // 512)),
                  pl.BlockSpec((tm, 512), tab), pl.BlockSpec((tm, 512), tab),
                  full(1, 512), full(1, 512), full(LANES, LANES)],
        out_specs=[pl.BlockSpec((tm, 512), lambda i: (i, 0)), pl.BlockSpec((tm, 512), lambda i: (i, 0)),
                   pl.BlockSpec((A_HEADS, 1, A_DV + V_PAD, tm), lambda i: (0, i, 0, 0))],
        out_shape=[jax.ShapeDtypeStruct((rows, 512), BF16), jax.ShapeDtypeStruct((rows, 512), BF16),
                   jax.ShapeDtypeStruct((A_HEADS, rows // tm, A_DV + V_PAD, tm), BF16)],
        compiler_params=pltpu.CompilerParams(dimension_semantics=("arbitrary",)),
        name='attn_prep',
    )(z, z, z, cos_t, sin_t, qn, kn, _group_ones(A_DH))


def _attn_kernel(*refs, n_lat, lam_init):
    lam_ref, sub_ref, q_ref, kc_ref, vc_ref = refs[:5]
    if n_lat:
        kl_ref, vl_ref = refs[5:7]
        o_ref, m_scr, acc_scr, *s_scr = refs[7:]
    else:
        o_ref, m_scr, acc_scr = refs[5:]

    q = q_ref[...]
    lane = lax.broadcasted_iota(jnp.int32, (1, LANES), 1)
    zero = jnp.zeros_like(q)
    qs = (jnp.where(lane < A_DH, q, zero), jnp.where(lane >= A_DH, q, zero))
    m_scr[...] = jnp.full_like(m_scr, -jnp.inf)
    acc_scr[...] = jnp.zeros_like(acc_scr)

    def scores(kb):
        return [_dot_nt(kb, qm) for qm in qs]

    def absorb(s, vt):
        m_old = [m_scr[i] for i in range(2)]
        m_new = [jnp.maximum(m_old[i], jnp.max(s[i], axis=0, keepdims=True)) for i in range(2)]
        p = [jnp.exp2(s[i] - m_new[i]).astype(BF16) for i in range(2)]
        pv = [_dot(vt, p[i]) for i in range(2)]
        for i in range(2):
            acc_scr[i] = jnp.exp2(m_old[i] - m_new[i]) * acc_scr[i] + pv[i]
            m_scr[i] = m_new[i]

    absorb(scores(kc_ref[...]), vc_ref[0, 0])
    if n_lat:
        def keys(c):
            return kl_ref[pl.ds(pl.multiple_of(c * KV_TILE, KV_TILE), KV_TILE), :]

        def put(scr, s):
            scr[0] = s[0]
            scr[1] = s[1]

        nbuf = len(s_scr)
        ahead = nbuf - ATT_LOOKBACK
        for u in range(ahead):
            put(s_scr[u], scores(keys(u)))

        def body(j, carry):
            c = nbuf * j
            for u in range(nbuf):
                put(s_scr[(u + ahead) % nbuf], scores(keys(jnp.minimum(c + u + ahead, n_lat - 1))))
                absorb([s_scr[u][0], s_scr[u][1]], vl_ref[0, c + u])
            return carry
        lax.fori_loop(0, n_lat // nbuf, body, 0)

    lp = lam_ref[...]
    lam = (jnp.exp(jnp.sum(lp[0:1, :] * lp[1:2, :], axis=1, keepdims=True))
           - jnp.exp(jnp.sum(lp[2:3, :] * lp[3:4, :], axis=1, keepdims=True)) + lam_init)
    a0, a1 = acc_scr[0], acc_scr[1]
    o = a0[:A_DV] / a0[A_DV:A_DV + 1] - lam * (a1[:A_DV] / a1[A_DV:A_DV + 1])
    ms = jnp.mean(o * o, axis=0, keepdims=True)
    o = o * lax.rsqrt(ms + NORM_EPS) * sub_ref[...] * (1.0 - lam_init)
    o_ref[...] = o.T


def _attention(qr, kr, vt, lam_p, subln, lam_init, geom, batch, ctx_queries):
    nl, seq, ctx = geom
    ctx_blk = nl // ctx
    per_tile = KV_TILE // ctx
    full = lambda *s: pl.BlockSpec(s, lambda b, h, i: (0,) * len(s))
    kv_ctx = [pl.BlockSpec((ctx, LANES), lambda b, h, i: (ctx_blk + b, h)),
              pl.BlockSpec((1, 1, A_DV + V_PAD, ctx),
                           lambda b, h, i: (h, nl // KV_TILE + b // per_tile, 0, lax.rem(b, per_tile)))]
    head = [full(4, A_DH), full(A_DV, 1)]
    scratch = lambda tq: [pltpu.VMEM((2, 1, tq), F32), pltpu.VMEM((2, A_DV + V_PAD, tq), F32)]
    if ctx_queries:
        tq, nq, n_lat, out_rows = ctx, 1, 0, batch * ctx
        in_specs = head + [pl.BlockSpec((tq, LANES), lambda b, h, i: (ctx_blk + b, h))] + kv_ctx
        args = (lam_p, subln.reshape(A_DV, 1), qr, kr, vt)
        omap = lambda b, h, i: (b, h)
        scratch_shapes = scratch(tq)
    else:
        tq, out_rows = Q_TILE, nl
        nq, n_lat = seq // tq, seq // KV_TILE
        assert n_lat % ATT_BUFS == 0
        omap = lambda b, h, i: (b * nq + i, h)
        in_specs = head + [pl.BlockSpec((tq, LANES), omap)] + kv_ctx + [
            pl.BlockSpec((seq, LANES), lambda b, h, i: (b, h)),
            pl.BlockSpec((1, n_lat, A_DV + V_PAD, KV_TILE), lambda b, h, i: (h, b, 0, 0))]
        args = (lam_p, subln.reshape(A_DV, 1), qr, kr, vt, kr, vt)
        scratch_shapes = scratch(tq) + [pltpu.VMEM((2, KV_TILE, tq), F32)] * ATT_BUFS
    return pl.pallas_call(
        functools.partial(_attn_kernel, n_lat=n_lat, lam_init=lam_init),
        grid=(batch, A_HEADS, nq),
        in_specs=in_specs,
        out_specs=pl.BlockSpec((tq, LANES), omap),
        out_shape=jax.ShapeDtypeStruct((out_rows, MIX_W), F32),
        scratch_shapes=scratch_shapes,
        compiler_params=pltpu.CompilerParams(dimension_semantics=("arbitrary", "arbitrary", "arbitrary")),
        name='diff_attn_ctx' if ctx_queries else 'diff_attn',
    )(*args)


def _merge_kernel(x_ref, mod_ref, hm_ref, mo_ref, on_ref, yr_ref, ya_ref, g_ref, wb_ref, wo_ref, o_ref):
    hm = hm_ref[0] + hm_ref[1]
    on = on_ref[...]
    parts = []
    for h in range(M_HEADS):
        blk = hm[:, h * M_DV:(h + 1) * M_DV]
        ms = jnp.mean(blk * blk, axis=-1, keepdims=True)
        parts.append(blk * lax.rsqrt(ms + NORM_EPS) * on[:, h * M_DV:(h + 1) * M_DV])
    ym = jnp.concatenate(parts, axis=1) * _sigmoid(mo_ref[...].astype(F32))
    d = x_ref.shape[1]
    zsum = _sigmoid(g_ref[:, 0:d].astype(F32)) *_dot(ym.astype(BF16), wb_ref[0])
    zsum += _sigmoid(g_ref[:, d:2 * d].astype(F32)) *_dot(yr_ref[...].astype(BF16), wb_ref[1])
    zsum += _sigmoid(g_ref[:, 2 * d:3 * d].astype(F32)) *_dot(ya_ref[...].astype(BF16), wb_ref[2])
    o_ref[...] = x_ref[...] + mod_ref[0, 5:6, :] * _dot(zsum.astype(BF16), wo_ref[...])


def _merge(xr, mods, hm, z, out_norm, yr, ya, w_branch, w_o, rows, stream_of_tile):
    d = xr.shape[1]
    tm = ROW_TILE
    full = lambda *s: pl.BlockSpec(s, lambda i: (0,) * len(s))
    return pl.pallas_call(
        _merge_kernel,
        grid=(rows // tm,),
        in_specs=[pl.BlockSpec((tm, d), lambda i: (i, 0)),
                  pl.BlockSpec((1, N_ADA, d), lambda i: (stream_of_tile(i), 0, 0)),
                  pl.BlockSpec((2, tm, MIX_W), lambda i: (0, i, 0)),
                  pl.BlockSpec((tm, MIX_W), lambda i: (i, Z_MO // MIX_W)),
                  full(1, MIX_W),
                  pl.BlockSpec((tm, MIX_W), lambda i: (i, 0)),
                  pl.BlockSpec((tm, MIX_W), lambda i: (i, 0)),
                  pl.BlockSpec((tm, 3 * d), lambda i: (i, Z_G // (3 * d))),
                  full(3, MIX_W, d), full(d, d)],
        out_specs=pl.BlockSpec((tm, d), lambda i: (i, 0)),
        out_shape=jax.ShapeDtypeStruct((rows, d), F32),
        compiler_params=pltpu.CompilerParams(dimension_semantics=("arbitrary",)),
        name='merge',
    )(xr, mods, hm, z, out_norm.reshape(1, MIX_W), yr, ya, z, w_branch, w_o)


def _lambda_init(layer):
    return 0.8 - 0.6 * math.exp(-0.3 * layer)


def kernel(x, c, ctx, c_ctx, w_ada, b_ada, norm_g, ffn1_w_gu, ffn1_w_down, ffn2_w_gu, ffn2_w_down,
           w_in, m_conv, m_gate_bias, m_out_norm, r_conv, r_w0, r_w2, r_a0, r_a2, r_g2, r_kk, r_ka,
           r_rk, r_ln_w, r_ln_b, a_qk_norm, a_lambda, a_subln, w_branch, w_o):
    batch, seq, d = x.shape
    ctx_len = ctx.shape[1]
    depth = w_ada.shape[0]
    nl, nc = batch * seq, batch * ctx_len
    rows = nl + nc
    geom = (nl, seq, ctx_len)
    tm = ROW_TILE
    assert seq % tm == 0 and nc % tm == 0 and KV_TILE == tm and KV_TILE % ctx_len == 0 and nl % ctx_len == 0
    assert ctx_len % CHUNK == 0 and seq % GRID_W == 0 and d == 1024

    tiles_per_seq = seq // tm
    stream_of_tile = lambda i: jnp.minimum(i // tiles_per_seq, batch)

    cv = jnp.zeros((SUBLANES, d), F32).at[:batch].set(c).at[batch].set(c_ctx)
    mods_all = _ada_all(cv, w_ada, b_ada)[:, :batch + 1].reshape(depth, batch + 1, N_ADA, d)

    zcols = _z_column_sources()
    zsrc = jnp.asarray(np.maximum(zcols, 0))
    zmask = jnp.asarray((zcols >= 0).astype(np.float32))
    cos_t, sin_t = _rope_tables(seq, tm)

    xr = jnp.concatenate([x.reshape(nl, d), ctx.reshape(nc, d)], axis=0)
    for li in range(depth):
        last = li == depth - 1
        mods = mods_all[li]
        w_z = (jnp.take(w_in[li], zsrc, axis=1) * zmask).astype(BF16)
        w_z = w_z.reshape(d, Z_W // Z_TILE, Z_TILE).transpose(1, 0, 2)

        xr = _ffn(xr, mods, norm_g[li, 0], ffn1_w_gu[li].astype(BF16), ffn1_w_down[li].astype(BF16),
                  0, rows, stream_of_tile)
        z = _inproj(xr, mods, norm_g[li, 1], w_z, stream_of_tile)

        qk_act, gcol = _mprep(z, m_conv[li], m_gate_bias[li], geom)
        qk_hm = qk_act.reshape(rows, 2 * M_HEADS, M_DQK).transpose(1, 0, 2)
        grow = gcol[:, :16].reshape(rows // CHUNK, CHUNK, 16).transpose(0, 2, 1)
        hm = _mlstm_scan(qk_hm, z, gcol, grow, geom, batch)

        rkv_c, g_r, vb, *scan_in = _rprep(z, r_conv[li], r_w0[li], r_w2[li], r_a0[li], r_a2[li], r_g2[li],
                                           r_kk[li], r_ka[li], geom)
        y_tm = _rwkv_scan(*scan_in[:6], vb, scan_in[6], geom, batch)
        yr = _rpost(y_tm, rkv_c, g_r, r_ln_w[li], r_ln_b[li], r_rk[li])

        qr, kr, vt = _aprep(z, cos_t, sin_t, a_qk_norm[li], geom)
        att = lambda cq: _attention(qr, kr, vt, a_lambda[li], a_subln[li], _lambda_init(li), geom, batch, cq)
        ya = att(False) if last else jnp.concatenate([att(False), att(True)], axis=0)

        out_rows = nl if last else rows
        xr = _merge(xr, mods, hm, z, m_out_norm[li], yr, ya, w_branch[li].astype(BF16),
                    w_o[li].astype(BF16), out_rows, stream_of_tile)
        xr = _ffn(xr, mods, norm_g[li, 2], ffn2_w_gu[li].astype(BF16), ffn2_w_down[li].astype(BF16),
                  6, out_rows, stream_of_tile)
    return xr[:nl].reshape(batch, seq, d)
```

```python
import functools
import math

import numpy as np
import jax
import jax.numpy as jnp
from jax import lax
from jax.experimental import pallas as pl
from jax.experimental.pallas import tpu as pltpu

F32 = jnp.float32
BF16 = jnp.bfloat16
HI = lax.Precision.HIGHEST

N_ADA = 9
MACARON_W = 0.5
NORM_EPS = 1e-6
MIX_W = 512
GRID_W = 64
CHUNK = 64
SCAN_CHUNKS = 4
M_SCAN_CHUNKS = 2

M_HEADS = 4
M_DQK = 64
M_DV = 128
R_HEADS = 8
R_DH = 64
R_GN_EPS = 64e-5
A_HEADS = 4
A_DH = 64
A_DV = 128
ROPE_BASE = 10000.0

LANES = 128
SUBLANES = 8
HALO_ROWS = 16
ROW_TILE = 512
RWKV_PREP_TILE = 256
Q_TILE = 512
KV_TILE = 512
V_PAD = 16
LOG2_E = math.log2(math.e)
ATT_BUFS = 4
ATT_LOOKBACK = 2
FFN_CHUNK = 256
FFN_VMEM_LIMIT = 48 * 2 ** 20

_IN_SPLITS = (
    ('m_q', 256), ('m_k', 256), ('m_v', 512), ('m_o', 512),
    ('m_if', 4), ('m_ff', 4), ('m_ib', 4), ('m_fb', 4),
    ('r_r', 512), ('r_k', 512), ('r_v', 512),
    ('r_wf', 64), ('r_wb', 64), ('r_af', 64), ('r_ab', 64), ('r_g', 128),
    ('a_q', 512), ('a_k', 512), ('a_v', 512),
    ('g_m', 1024), ('g_r', 1024), ('g_a', 1024),
)
Z_G, Z_MQK, Z_MV, Z_MO, Z_RKV, Z_AQ, Z_AK, Z_AV, Z_LORA, Z_MG, Z_W = (
    0, 3072, 3584, 4096, 4608, 6144, 6656, 7168, 7680, 8064, 8192)
Z_TILE = 2048


def _deinterleave64():
    return np.concatenate([np.arange(0, 64, 2), np.arange(1, 64, 2)])


def _z_column_sources():
    off, start = {}, 0
    for name, w in _IN_SPLITS:
        off[name] = start
        start += w
    cols = []
    rng = lambda n: list(range(off[n], off[n] + dict(_IN_SPLITS)[n]))
    cols += rng('g_m') + rng('g_r') + rng('g_a')
    cols += rng('m_q') + rng('m_k') + rng('m_v') + rng('m_o')
    cols += rng('r_r') + rng('r_k') + rng('r_v')
    perm = _deinterleave64()
    for n in ('a_q', 'a_k'):
        for g in range(8):
            cols += list(off[n] + g * 64 + perm)
    cols += rng('a_v')
    cols += rng('r_wf') + rng('r_wb') + rng('r_af') + rng('r_ab') + rng('r_g')
    cols += rng('m_if') + rng('m_ff') + rng('m_ib') + rng('m_fb')
    cols += [-1] * (Z_W - len(cols))
    assert len(cols) == Z_W
    return np.asarray(cols, np.int32)


def _sigmoid(x):
    return 1.0 / (1.0 + jnp.exp(-x))


def _dot(a, b, prec=None):
    return jnp.dot(a, b, preferred_element_type=F32, precision=prec)


def _dot_nt(a, b, prec=None):
    return lax.dot_general(a, b, (((1,), (1,)), ((), ())), preferred_element_type=F32, precision=prec)


def _dot_tn(a, b, prec=None):
    return lax.dot_general(a, b, (((0,), (0,)), ((), ())), preferred_element_type=F32, precision=prec)


def _norm_mod(x, g, shift, scale):
    ms = jnp.mean(x * x, axis=-1, keepdims=True)
    y = x * lax.rsqrt(ms + NORM_EPS) * g
    return y * (1.0 + scale) + shift


def _ada_kernel(c_ref, w_ref, b_ref, o_ref):
    c = c_ref[...]
    s = c * _sigmoid(c)
    o_ref[0] = _dot(s, w_ref[0], HI) + b_ref[0]


def _ada_all(cv, w_ada, b_ada):
    depth, d, nd = w_ada.shape
    tn = 1024
    return pl.pallas_call(
        _ada_kernel,
        grid=(depth, nd // tn),
        in_specs=[pl.BlockSpec((SUBLANES, d), lambda l, n: (0, 0)),
                  pl.BlockSpec((1, d, tn), lambda l, n: (l, 0, n)),
                  pl.BlockSpec((1, 1, tn), lambda l, n: (l, 0, n))],
        out_specs=pl.BlockSpec((1, SUBLANES, tn), lambda l, n: (l, 0, n)),
        out_shape=jax.ShapeDtypeStruct((depth, SUBLANES, nd), F32),
        name='ada',
    )(cv, w_ada, b_ada.reshape(depth, 1, nd))


def _ffn_kernel(x_ref, mod_ref, g_ref, wgu_ref, wd_ref, o_ref, u_scr, *, mi):
    dff = wd_ref.shape[0]
    h = _norm_mod(x_ref[...], g_ref[...], mod_ref[0, mi:mi + 1, :], mod_ref[0, mi + 1:mi + 2, :]).astype(BF16)
    for j in range(dff // FFN_CHUNK):
        lo, hi = j * FFN_CHUNK, (j + 1) * FFN_CHUNK
        a = _dot(h, wgu_ref[:, lo:hi])
        b = _dot(h, wgu_ref[:, dff + lo:dff + hi])
        u_scr[:, lo:hi] = ((a * _sigmoid(a)) * b).astype(BF16)
    o_ref[...] = x_ref[...] + MACARON_W * mod_ref[0, mi + 2:mi + 3, :] * _dot(u_scr[...], wd_ref[...])


def _ffn(xr, mods, g, w_gu, w_down, mi, rows, stream_of_tile):
    d = xr.shape[1]
    dff = w_down.shape[0]
    assert dff % FFN_CHUNK == 0
    tm = ROW_TILE
    resident = lambda *s: pl.BlockSpec(s, lambda i: (0,) * len(s), pipeline_mode=pl.Buffered(1))
    return pl.pallas_call(
        functools.partial(_ffn_kernel, mi=mi),
        grid=(rows // tm,),
        in_specs=[pl.BlockSpec((tm, d), lambda i: (i, 0)),
                  pl.BlockSpec((1, N_ADA, d), lambda i: (stream_of_tile(i), 0, 0)),
                  pl.BlockSpec((1, d), lambda i: (0, 0)),
                  resident(d, 2 * dff), resident(dff, d)],
        out_specs=pl.BlockSpec((tm, d), lambda i: (i, 0)),
        out_shape=jax.ShapeDtypeStruct((rows, d), F32),
        scratch_shapes=[pltpu.VMEM((tm, dff), BF16)],
        compiler_params=pltpu.CompilerParams(dimension_semantics=("arbitrary",),
                                             vmem_limit_bytes=FFN_VMEM_LIMIT),
        name='ffn',
    )(xr, mods, g.reshape(1, d), w_gu, w_down)


def _inproj_kernel(x_ref, mod_ref, g_ref, w_ref, o_ref, h_scr):
    @pl.when(pl.program_id(1) == 0)
    def _():
        h = _norm_mod(x_ref[...], g_ref[...], mod_ref[0, 3:4, :], mod_ref[0, 4:5, :])
        h_scr[...] = h.astype(BF16)

    o_ref[...] = _dot(h_scr[...], w_ref[pl.program_id(1)]).astype(o_ref.dtype)


def _inproj(xr, mods, g, w_z, stream_of_tile):
    rows, d = xr.shape
    tm = ROW_TILE
    nz = Z_W // Z_TILE
    return pl.pallas_call(
        _inproj_kernel,
        grid=(rows // tm, nz),
        in_specs=[pl.BlockSpec((tm, d), lambda i, n: (i, 0)),
                  pl.BlockSpec((1, N_ADA, d), lambda i, n: (stream_of_tile(i), 0, 0)),
                  pl.BlockSpec((1, d), lambda i, n: (0, 0)),
                  pl.BlockSpec((nz, d, Z_TILE), lambda i, n: (0, 0, 0), pipeline_mode=pl.Buffered(1))],
        out_specs=pl.BlockSpec((tm, Z_TILE), lambda i, n: (i, n)),
        out_shape=jax.ShapeDtypeStruct((rows, Z_W), BF16),
        scratch_shapes=[pltpu.VMEM((tm, d), BF16)],
        compiler_params=pltpu.CompilerParams(dimension_semantics=("arbitrary", "arbitrary"),
                                             vmem_limit_bytes=FFN_VMEM_LIMIT),
        name='inproj',
    )(xr, mods, g.reshape(1, d), w_z)


def _conv3(x, prev_rows, next_rows, w, row0, geom):
    nl, seq, ctx = geom
    tm = x.shape[0]
    x = x.astype(F32)
    t = lax.broadcasted_iota(jnp.int32, (tm, 1), 0)
    r = row0 + t
    is_lat = r < nl
    pos = jnp.where(is_lat, lax.rem(r, seq), lax.rem(jnp.maximum(r - nl, 0), ctx))
    seglen = jnp.where(is_lat, seq, ctx)
    xm = pltpu.roll(x, 1, 0)
    xm = jnp.where(t == 0, prev_rows[HALO_ROWS - 1:HALO_ROWS, :].astype(F32), xm)
    xm = jnp.where(pos == 0, 0.0, xm)
    xp = pltpu.roll(x, tm - 1, 0)
    xp = jnp.where(t == tm - 1, next_rows[0:1, :].astype(F32), xp)
    xp = jnp.where(pos == seglen - 1, 0.0, xp)
    return xm * w[0:1, :] + x * w[1:2, :] + xp * w[2:3, :]


def _halo_specs(tm, width, col_blk, rows):
    per = tm // HALO_ROWS
    last = rows // HALO_ROWS - 1
    return [pl.BlockSpec((tm, width), lambda i: (i, col_blk)),
            pl.BlockSpec((HALO_ROWS, width), lambda i: (jnp.maximum(i * per - 1, 0), col_blk)),
            pl.BlockSpec((HALO_ROWS, width), lambda i: (jnp.minimum((i + 1) * per, last), col_blk))]


def _mprep_kernel(x_ref, xp_ref, xn_ref, w_ref, gt_ref, gb_ref, qk_ref, go_ref, *, geom):
    tm = x_ref.shape[0]
    y = _conv3(x_ref[...], xp_ref[...], xn_ref[...], w_ref[...], pl.program_id(0) * tm, geom)
    y = y * _sigmoid(y)
    lane = lax.broadcasted_iota(jnp.int32, (1, y.shape[1]), 1)
    qk_ref[...] = jnp.where(lane >= M_HEADS * M_DQK, y * (M_DQK ** -0.5), y)
    g = gt_ref[...].astype(F32) + gb_ref[...]
    gl = lax.broadcasted_iota(jnp.int32, (1, LANES), 1)
    is_forget = (lax.rem(gl, 2 * M_HEADS) >= M_HEADS) & (gl < 4 * M_HEADS)
    logsig = jnp.minimum(g, 0.0) - jnp.log(1.0 + jnp.exp(-jnp.abs(g)))
    go_ref[...] = jnp.where(is_forget, logsig, g)


def _mprep(z, conv_w, gate_bias, geom):
    rows = z.shape[0]
    tm = ROW_TILE
    gb = jnp.zeros((1, LANES), F32).at[0, :4 * M_HEADS].set(gate_bias.reshape(-1))
    return pl.pallas_call(
        functools.partial(_mprep_kernel, geom=geom),
        grid=(rows // tm,),
        in_specs=_halo_specs(tm, 512, Z_MQK // 512, rows) + [
            pl.BlockSpec((3, 512), lambda i: (0, 0)),
            pl.BlockSpec((tm, LANES), lambda i: (i, Z_MG // LANES)),
            pl.BlockSpec((1, LANES), lambda i: (0, 0))],
        out_specs=[pl.BlockSpec((tm, 512), lambda i: (i, 0)),
                   pl.BlockSpec((tm, LANES), lambda i: (i, 0))],
        out_shape=[jax.ShapeDtypeStruct((rows, 512), F32), jax.ShapeDtypeStruct((rows, LANES), F32)],
        compiler_params=pltpu.CompilerParams(dimension_semantics=("arbitrary",)),
        name='mlstm_prep',
    )(z, z, z, conv_w, z, gb)


def _chunk_block(d, b, j, geom, size=CHUNK):
    nl, seq, ctx = geom
    nc_c, nc_l = ctx // size, seq // size
    jl = j - nc_c
    c_ctx = jnp.where(d == 0, j, nc_c - 1 - j)
    c_lat = jnp.where(d == 0, jl, nc_l - 1 - jl)
    return jnp.where(j < nc_c, nl // size + b * nc_c + c_ctx, b * nc_l + c_lat)


def _order_masks(d):
    ti = lax.broadcasted_iota(jnp.int32, (CHUNK, CHUNK), 0)
    si = lax.broadcasted_iota(jnp.int32, (CHUNK, CHUNK), 1)
    fwd = d == 0
    ahead = jnp.where(fwd, si - ti, ti - si)
    incl = ahead <= 0
    strict = ahead < 0
    incl_t = ahead >= 0
    return fwd, incl, strict, incl_t, ti == si


def _mlstm_kernel(qk_ref, v_ref, gc_ref, gr_ref, o_ref, ct_scr, m_scr):
    d = pl.program_id(0)

    @pl.when(pl.program_id(2) == 0)
    def _():
        ct_scr[...] = jnp.zeros_like(ct_scr)
        m_scr[...] = jnp.zeros_like(m_scr)

    fwd, incl, _, incl_t, _ = _order_masks(d)
    heads = range(M_HEADS)
    chunks = range(M_SCAN_CHUNKS)
    lane = lax.broadcasted_iota(jnp.int32, (CHUNK, LANES), 1)
    ones_col = jnp.where(lane == 0, 1.0, 0.0)
    cat1 = lambda xs: jnp.concatenate(xs, axis=1)

    pre = []
    for u in chunks:
        cu = jnp.where(fwd, u, M_SCAN_CHUNKS - 1 - u)
        rows = pl.ds(pl.multiple_of(cu * CHUNK, CHUNK), CHUNK)
        gc = gc_ref[rows, :]
        gr = gr_ref[cu]
        q = [qk_ref[h, rows, :].astype(BF16) for h in heads]
        k = [qk_ref[M_HEADS + h, rows, :].astype(BF16) for h in heads]
        v = [v_ref[rows, h * M_DV:(h + 1) * M_DV].astype(F32) for h in heads]
        qk = [_dot_nt(q[h], k[h]) for h in heads]
        per_head = []
        for h in heads:
            i_col = jnp.where(fwd, gc[:, h:h + 1], gc[:, 8 + h:9 + h])
            f_col = jnp.where(fwd, gc[:, 4 + h:5 + h], gc[:, 12 + h:13 + h])
            i_row = jnp.where(fwd, gr[h:h + 1, :], gr[8 + h:9 + h, :])
            f_row = jnp.where(fwd, gr[4 + h:5 + h, :], gr[12 + h:13 + h, :])
            bcum_col = jnp.sum(jnp.where(incl, f_row, 0.0), axis=1, keepdims=True)
            bcum_row = jnp.sum(jnp.where(incl_t, f_col, 0.0), axis=0, keepdims=True)
            dlog = jnp.where(incl, bcum_col - bcum_row + i_row, -jnp.inf)
            rmax = jnp.max(dlog, axis=1, keepdims=True)
            btot = jnp.sum(f_col, axis=0, keepdims=True)
            wlog = btot - bcum_col + i_col
            per_head.append(dict(bcum=bcum_col, rmax=rmax, btot=btot, wlog=wlog,
                                 wlog_row=btot - bcum_row + i_row,
                                 wmax=jnp.max(wlog, axis=0, keepdims=True),
                                 qkd=(qk[h] * jnp.exp(dlog - rmax)).astype(BF16)))
        v_aug = [cat1([v[h], ones_col]) for h in heads]
        v_aug_b = [a.astype(BF16) for a in v_aug]
        sv0 = [_dot(per_head[h]['qkd'], v_aug_b[h]) for h in heads]
        cross = [[_dot_nt(q[h], p['k'][h]) for h in heads] for p in pre]
        pre.append(dict(rows=rows, q=q, k=k, v_aug=v_aug, v_aug_b=v_aug_b, sv0=sv0, cross=cross, g=per_head))

    c0 = [ct_scr[h] for h in heads]
    c0b = [a.astype(BF16) for a in c0]
    m_st = [m_scr[h] for h in heads]
    ct = list(c0)
    decay = [1.0 for _ in heads]
    earlier = []
    for u in chunks:
        c = pre[u]
        qc = [decay[h] * _dot(c['q'][h], c0b[h]) for h in heads]
        for e, w_rows in enumerate(earlier):
            for h in heads:
                qc[h] = qc[h] + _dot((c['cross'][e][h] * w_rows[h]).astype(BF16), pre[e]['v_aug_b'][h])
        m_new = [jnp.maximum(c['g'][h]['btot'] + m_st[h], c['g'][h]['wmax']) for h in heads]
        dec = [jnp.exp(c['g'][h]['btot'] + m_st[h] - m_new[h]) for h in heads]
        ws = [jnp.exp(c['g'][h]['wlog'] - m_new[h]) for h in heads]
        kv = [_dot_tn(c['k'][h], (ws[h] * c['v_aug'][h]).astype(BF16)) for h in heads]
        for h in heads:
            g = c['g'][h]
            inter = g['bcum'] + m_st[h]
            mt = jnp.maximum(inter, g['rmax'])
            nd = jnp.exp(g['rmax'] - mt) * c['sv0'][h] + jnp.exp(inter - mt) * qc[h]
            den = jnp.maximum(jnp.abs(nd[:, M_DV:M_DV + 1]), jnp.exp(-mt))
            o_ref[0, c['rows'], h * M_DV:(h + 1) * M_DV] = nd[:, :M_DV] / den
            ct[h] = dec[h] * ct[h] + kv[h]
        earlier = [[w_rows[h] * dec[h] for h in heads] for w_rows in earlier]
        earlier.append([jnp.exp(c['g'][h]['wlog_row'] - m_new[h]) for h in heads])
        decay = [decay[h] * dec[h] for h in heads]
        m_st = m_new
    for h in heads:
        ct_scr[h] = ct[h]
        m_scr[h] = m_st[h]


def _mlstm_scan(qk_hm, z, gcol, grow, geom, batch):
    rows = z.shape[0]
    step = M_SCAN_CHUNKS * CHUNK
    assert geom[1] % step == 0 and geom[2] % step == 0
    nch = (geom[1] + geom[2]) // step
    blk = lambda d, b, j: _chunk_block(d, b, j, geom, step)
    return pl.pallas_call(
        _mlstm_kernel,
        grid=(2, batch, nch),
        in_specs=[pl.BlockSpec((2 * M_HEADS, step, M_DQK), lambda d, b, j: (0, blk(d, b, j), 0)),
                  pl.BlockSpec((step, MIX_W), lambda d, b, j: (blk(d, b, j), Z_MV // MIX_W)),
                  pl.BlockSpec((step, LANES), lambda d, b, j: (blk(d, b, j), 0)),
                  pl.BlockSpec((M_SCAN_CHUNKS, 16, CHUNK), lambda d, b, j: (blk(d, b, j), 0, 0))],
        out_specs=pl.BlockSpec((1, step, MIX_W), lambda d, b, j: (d, blk(d, b, j), 0)),
        out_shape=jax.ShapeDtypeStruct((2, rows, MIX_W), F32),
        scratch_shapes=[pltpu.VMEM((M_HEADS, M_DQK, M_DV + LANES), F32),
                        pltpu.VMEM((M_HEADS, 1, 1), F32)],
        compiler_params=pltpu.CompilerParams(dimension_semantics=("arbitrary", "arbitrary", "arbitrary")),
        name='mlstm_scan',
    )(qk_hm, z, gcol, grow)


def _rprep_kernel(x_ref, xp_ref, xn_ref, cw_ref, lo_ref, w0_ref, w2_ref, a0_ref, a2_ref, g2_ref,
                  kkw_ref, kaw_ref, bd_ref, tri_ref,
                  rkv_ref, g_ref, vb_ref, at_ref, bt_ref, kt_ref, rt_ref, bb_ref, kb_ref, et_ref, *, geom):
    tm = x_ref.shape[0]
    rkv = _conv3(x_ref[...], xp_ref[...], xn_ref[...], cw_ref[...], pl.program_id(0) * tm, geom)
    rkv_ref[...] = rkv
    r = rkv[:, 0:MIX_W]
    k = rkv[:, MIX_W:2 * MIX_W]
    vb_ref[...] = rkv[:, 2 * MIX_W:3 * MIX_W].astype(BF16)
    lo = lo_ref[...].astype(F32)
    g_ref[...] = _dot(_sigmoid(lo[:, 256:384]), g2_ref[...], HI)
    kkr = k * kkw_ref[...]
    kk = kkr / jnp.maximum(jnp.sqrt(_group_sum(kkr * kkr, bd_ref[...])), 1e-12)
    for d in range(2):
        w_raw = w0_ref[d:d + 1, :] + _dot(jnp.tanh(lo[:, 64 * d:64 * d + 64]), w2_ref[d], HI)
        lw = -_sigmoid(w_raw) * math.exp(-0.5)
        a = _sigmoid(a0_ref[d:d + 1, :] + _dot(lo[:, 128 + 64 * d:192 + 64 * d], a2_ref[d], HI))
        hi = lw.astype(BF16)
        rem = lw - hi.astype(F32)
        mid = rem.astype(BF16)
        low = (rem - mid.astype(F32)).astype(BF16)
        parts = jnp.concatenate([hi, mid, low], axis=1)
        c3 = _dot(tri_ref[d, 0], parts)
        s3 = _dot(tri_ref[d, 1], parts)
        cum = c3[:, 0:MIX_W] + c3[:, MIX_W:2 * MIX_W] + c3[:, 2 * MIX_W:3 * MIX_W]
        suf = s3[:, 0:MIX_W] + s3[:, MIX_W:2 * MIX_W] + s3[:, 2 * MIX_W:3 * MIX_W]
        kd = k * (1.0 + (a - 1.0) * kaw_ref[...])
        kka = kk * a
        e_neg = jnp.exp(-cum)
        e_end = jnp.exp(suf)
        at_ref[d] = (-kk * jnp.exp(cum - lw)).astype(BF16)
        bt_ref[d] = (kka * e_neg).astype(BF16)
        kt_ref[d] = (kd * e_neg).astype(BF16)
        rt_ref[d] = (r * jnp.exp(cum)).astype(BF16)
        bb_ref[d] = (kka * e_end).astype(BF16)
        kb_ref[d] = (kd * e_end).astype(BF16)
        et_ref[d] = jnp.exp(cum + suf)


def _chunk_order_matrices(tm):
    i = np.arange(tm)
    same = (i[:, None] // CHUNK) == (i[None, :] // CHUNK)
    le = i[None, :] <= i[:, None]
    ge = i[None, :] >= i[:, None]
    mats = np.stack([np.stack([same & le, same & ~le]), np.stack([same & ge, same & ~ge])])
    return jnp.asarray(mats.astype(np.float32), dtype=BF16)


def _rprep(z, conv_w, w0, w2, a0, a2, g2, kk_w, ka_w, geom):
    rows = z.shape[0]
    tm = RWKV_PREP_TILE
    full = lambda *s: pl.BlockSpec(s, lambda i: (0,) * len(s))
    row = lambda w: pl.BlockSpec((tm, w), lambda i: (i, 0))
    both = pl.BlockSpec((2, tm, MIX_W), lambda i: (0, i, 0))
    shp = lambda dt: jax.ShapeDtypeStruct((2, rows, MIX_W), dt)
    return pl.pallas_call(
        functools.partial(_rprep_kernel, geom=geom),
        grid=(rows // tm,),
        in_specs=_halo_specs(tm, 1536, Z_RKV // 1536, rows) + [
            full(3, 1536),
            pl.BlockSpec((tm, 384), lambda i: (i, Z_LORA // 384)),
            full(2, 512), full(2, 64, 512), full(2, 512), full(2, 64, 512), full(128, 512),
            full(1, MIX_W), full(1, MIX_W), full(LANES, LANES), full(2, 2, tm, tm)],
        out_specs=[row(1536), row(MIX_W), row(MIX_W)] + [both] * 7,
        out_shape=[jax.ShapeDtypeStruct((rows, 1536), F32), jax.ShapeDtypeStruct((rows, MIX_W), F32),
                   jax.ShapeDtypeStruct((rows, MIX_W), BF16)] + [shp(BF16)] * 6 + [shp(F32)],
        compiler_params=pltpu.CompilerParams(dimension_semantics=("arbitrary",)),
        name='rwkv_prep',
    )(z, z, z, conv_w, z, w0, w2, a0, a2, g2, kk_w.reshape(1, MIX_W), ka_w.reshape(1, MIX_W),
      _group_ones(R_DH), _chunk_order_matrices(tm))


def _rwkv_kernel(at_ref, bt_ref, kt_ref, rt_ref, bb_ref, kb_ref, v_ref, et_ref, y_ref, ss_scr):
    d = pl.program_id(0)

    @pl.when(pl.program_id(2) == 0)
    def _():
        ss_scr[...] = jnp.zeros_like(ss_scr)

    ti = lax.broadcasted_iota(jnp.int32, (CHUNK, LANES), 0)
    li = lax.broadcasted_iota(jnp.int32, (CHUNK, LANES), 1)
    si = jnp.bitwise_and(li, R_DH - 1)
    ahead = jnp.where(d == 0, si - ti, ti - si)
    strict2 = ahead < 0
    incl2 = ahead <= 0
    first = li < R_DH
    eye64 = (lax.broadcasted_iota(jnp.int32, (CHUNK, R_DH), 0)
             == lax.broadcasted_iota(jnp.int32, (CHUNK, R_DH), 1)).astype(F32)
    rr = lax.broadcasted_iota(jnp.int32, (LANES, LANES), 0)
    cc = lax.broadcasted_iota(jnp.int32, (LANES, LANES), 1)
    same_head = (rr < R_DH) == (cc < R_DH)
    eye128 = rr == cc
    zero = jnp.zeros((CHUNK, LANES), BF16)
    cat0 = lambda xs: jnp.concatenate(xs, axis=0)
    cat1 = lambda xs: jnp.concatenate(xs, axis=1)
    keep = lambda h, x: jnp.where(first, x, zero) if h == 0 else jnp.where(first, zero, x)

    pairs = range(R_HEADS // 2)
    items = [(u, p) for u in range(SCAN_CHUNKS) for p in pairs]
    rows_of = [pl.ds(pl.multiple_of(jnp.where(d == 0, u, SCAN_CHUNKS - 1 - u) * CHUNK, CHUNK), CHUNK)
               for u in range(SCAN_CHUNKS)]
    lanes_of = [slice(p * LANES, (p + 1) * LANES) for p in pairs]
    at = [at_ref[0, rows_of[u], lanes_of[p]] for u, p in items]
    rt = [rt_ref[0, rows_of[u], lanes_of[p]] for u, p in items]
    v = [v_ref[rows_of[u], lanes_of[p]] for u, p in items]
    n = range(len(items))
    g = [_dot_nt(cat0([keep(0, at[i]), keep(0, rt[i]), keep(1, at[i]), keep(1, rt[i])]),
                 cat0([bt_ref[0, rows_of[u], lanes_of[p]], kt_ref[0, rows_of[u], lanes_of[p]]]))
         for i, (u, p) in enumerate(items)]
    ga = [[jnp.where(strict2, g[i][2 * h * CHUNK:(2 * h + 1) * CHUNK], 0.0) for h in range(2)] for i in n]
    gr = [[jnp.where(incl2, g[i][(2 * h + 1) * CHUNK:(2 * h + 2) * CHUNK], 0.0) for h in range(2)] for i in n]
    aakv = [_dot(cat1(ga[i]).astype(BF16), cat0([zero, keep(0, v[i]), zero, keep(1, v[i])])).astype(BF16)
            for i in n]
    pw = [ga[i][h][:, :R_DH] for i in n for h in range(2)]
    tinv = [eye64 + a for a in pw]
    for _ in range(int(math.log2(CHUNK)) - 1):
        pwb = [a.astype(BF16) for a in pw]
        pw = [_dot(a, a) for a in pwb]
        tinv = [t + _dot(t.astype(BF16), a.astype(BF16)) for t, a in zip(tinv, pw)]
    tinv = [t.astype(BF16) for t in tinv]
    x = [(_dot(tinv[2 * i], cat1([keep(0, at[i]), keep(0, aakv[i])]))
          + _dot(tinv[2 * i + 1], cat1([keep(1, at[i]), keep(1, aakv[i])]))).astype(BF16) for i in n]
    y4 = [_dot(cat1(gr[i]).astype(BF16),
               cat0([cat1([keep(0, x[i][:, :LANES]), keep(0, x[i][:, LANES:])]), cat1([zero, keep(0, v[i])]),
                     cat1([keep(1, x[i][:, :LANES]), keep(1, x[i][:, LANES:])]), cat1([zero, keep(1, v[i])])]))
          for i in n]
    mn = [_dot_tn(cat0([bb_ref[0, rows_of[u], lanes_of[p]], kb_ref[0, rows_of[u], lanes_of[p]]]),
                  cat0([x[i], cat1([zero, v[i]])])) for i, (u, p) in enumerate(items)]
    lhs = []
    for i, (u, p) in enumerate(items):
        rh = rt[i].astype(F32) + y4[i][:, :LANES]
        decay = et_ref[0, pl.ds(jnp.where(d == 0, u, SCAN_CHUNKS - 1 - u) * CHUNK, 1), lanes_of[p]]
        mt = jnp.where(eye128, decay, 0.0) + jnp.where(same_head, mn[i][:, :LANES], 0.0)
        lhs.append(cat0([rh, mt]).astype(BF16))
    ss = [ss_scr[p] for p in pairs]
    for u in range(SCAN_CHUNKS):
        out = [_dot(lhs[u * len(pairs) + p], ss[p].astype(BF16)) for p in pairs]
        for p in pairs:
            i = u * len(pairs) + p
            y_ref[0, rows_of[u], lanes_of[p]] = out[p][:CHUNK] + y4[i][:, LANES:]
            ss[p] = out[p][CHUNK:] + jnp.where(same_head, mn[i][:, LANES:], 0.0)
    for p in pairs:
        ss_scr[p] = ss[p]


def _rwkv_scan(at, bt, kt, rt, bb, kb, vb, et, geom, batch):
    rows = vb.shape[0]
    step = SCAN_CHUNKS * CHUNK
    assert geom[1] % step == 0 and geom[2] % step == 0
    nch = (geom[1] + geom[2]) // step
    blk = lambda d, b, j: _chunk_block(d, b, j, geom, step)
    per_dir = pl.BlockSpec((1, step, MIX_W), lambda d, b, j: (d, blk(d, b, j), 0))
    return pl.pallas_call(
        _rwkv_kernel,
        grid=(2, batch, nch),
        in_specs=[per_dir] * 6 + [pl.BlockSpec((step, MIX_W), lambda d, b, j: (blk(d, b, j), 0)), per_dir],
        out_specs=per_dir,
        out_shape=jax.ShapeDtypeStruct((2, rows, MIX_W), F32),
        scratch_shapes=[pltpu.VMEM((R_HEADS // 2, LANES, LANES), F32)],
        compiler_params=pltpu.CompilerParams(dimension_semantics=("arbitrary", "arbitrary", "arbitrary")),
        name='rwkv_scan',
    )(at, bt, kt, rt, bb, kb, vb, et)


def _rpost_kernel(y_ref, rkv_ref, g_ref, lnw_ref, lnb_ref, rk_ref, bd_ref, o_ref):
    y = y_ref[0] + y_ref[1]
    bd = bd_ref[...]
    inv = 1.0 / R_DH
    mu = _group_sum(y, bd) * inv
    yc = y - mu
    var = _group_sum(yc * yc, bd) * inv
    yn = yc * lax.rsqrt(var + R_GN_EPS) * lnw_ref[...] + lnb_ref[...]
    r = rkv_ref[:, 0:512]
    k = rkv_ref[:, 512:1024]
    v = rkv_ref[:, 1024:1536]
    bonus = _group_sum(r * k * rk_ref[...], bd) * v
    o_ref[...] = (yn + bonus) * g_ref[...]


def _group_ones(group):
    i = np.arange(LANES) // group
    return jnp.asarray((i[:, None] == i[None, :]).astype(np.float32), dtype=BF16)


def _group_sum(x, ones):
    tm, w = x.shape
    nt = w // LANES
    xs = jnp.concatenate([x[:, t * LANES:(t + 1) * LANES] for t in range(nt)], axis=0)
    hi = xs.astype(BF16)
    rem = xs - hi.astype(F32)
    mid = rem.astype(BF16)
    low = (rem - mid.astype(F32)).astype(BF16)
    y = _dot(jnp.concatenate([hi, mid, low], axis=0), ones)
    n = nt * tm
    ys = y[0:n] + y[n:2 * n] + y[2 * n:3 * n]
    return jnp.concatenate([ys[t * tm:(t + 1) * tm] for t in range(nt)], axis=1)


def _rpost(y_tm, rkv_c, g, ln_w, ln_b, r_k):
    rows = rkv_c.shape[0]
    tm = ROW_TILE
    full = lambda *s: pl.BlockSpec(s, lambda i: (0,) * len(s))
    return pl.pallas_call(
        _rpost_kernel,
        grid=(rows // tm,),
        in_specs=[pl.BlockSpec((2, tm, 512), lambda i: (0, i, 0)),
                  pl.BlockSpec((tm, 1536), lambda i: (i, 0)),
                  pl.BlockSpec((tm, 512), lambda i: (i, 0)),
                  full(1, 512), full(1, 512), full(1, 512), full(LANES, LANES)],
        out_specs=pl.BlockSpec((tm, 512), lambda i: (i, 0)),
        out_shape=jax.ShapeDtypeStruct((rows, 512), F32),
        compiler_params=pltpu.CompilerParams(dimension_semantics=("arbitrary",)),
        name='rwkv_post',
    )(y_tm, rkv_c, g, ln_w.reshape(1, 512), ln_b.reshape(1, 512), r_k.reshape(1, 512),
      _group_ones(R_DH))


def _aprep_kernel(q_ref, k_ref, v_ref, cos_ref, sin_ref, qn_ref, kn_ref, bd_ref, qo_ref, ko_ref, vt_ref):
    tm = v_ref.shape[0]
    pad_row = lax.broadcasted_iota(jnp.int32, (V_PAD, tm), 0)
    pad = jnp.where(pad_row == 0, 1.0, 0.0).astype(BF16)
    for h in range(A_HEADS):
        vt_ref[h, 0, 0:A_DV, :] = v_ref[:, h * A_DV:(h + 1) * A_DV].astype(F32).T.astype(BF16)
        vt_ref[h, 0, A_DV:A_DV + V_PAD, :] = pad
    bd = bd_ref[...]
    cos = cos_ref[...]
    sin = sin_ref[...]
    lane = lax.broadcasted_iota(jnp.int32, (1, 512), 1)
    lower = lax.rem(lane, A_DH) < A_DH // 2

    def one(x, g):
        ms = _group_sum(x * x, bd) * (1.0 / A_DH)
        y = x * lax.rsqrt(ms + NORM_EPS) * g
        swapped = jnp.where(lower, pltpu.roll(y, 512 - A_DH // 2, 1), pltpu.roll(y, A_DH // 2, 1))
        return y * cos + swapped * sin

    qo_ref[...] = one(q_ref[...].astype(F32), qn_ref[...]).astype(BF16)
    ko_ref[...] = one(k_ref[...].astype(F32), kn_ref[...]).astype(BF16)


def _rope_tables(seq, tm):
    rows = seq // GRID_W
    row = jnp.repeat(jnp.arange(rows, dtype=F32), GRID_W)
    col = jnp.tile(jnp.arange(GRID_W, dtype=F32), rows)
    half = A_DH // 2
    inv = ROPE_BASE ** (-jnp.arange(0, half, 2, dtype=F32) / half)
    ang = jnp.concatenate([row[:, None] * inv, col[:, None] * inv], axis=-1)
    cos, sin = jnp.cos(ang), jnp.sin(ang)
    cos64 = jnp.concatenate([cos, cos], axis=-1)
    sin64 = jnp.concatenate([-sin, sin], axis=-1)
    cos_t = jnp.concatenate([jnp.tile(cos64, (1, 8)), jnp.ones((tm, 512), F32)], axis=0)
    sin_t = jnp.concatenate([jnp.tile(sin64, (1, 8)), jnp.zeros((tm, 512), F32)], axis=0)
    return cos_t, sin_t


def _aprep(z, cos_t, sin_t, qk_norm, geom):
    rows = z.shape[0]
    nl, seq, _ = geom
    tm = ROW_TILE
    perm = _deinterleave64()
    qn = jnp.tile(qk_norm[0][perm] * (A_DH ** -0.5 * LOG2_E), 8).reshape(1, 512)
    kn = jnp.tile(qk_norm[1][perm], 8).reshape(1, 512)
    tiles_per_seq = seq // tm
    tab = lambda i: (jnp.where(i < nl // tm, lax.rem(i, tiles_per_seq), tiles_per_seq), 0)
    full = lambda *s: pl.BlockSpec(s, lambda i: (0,) * len(s))
    return pl.pallas_call(
        _aprep_kernel,
        grid=(rows // tm,),
        in_specs=[pl.BlockSpec((tm, 512), lambda i: (i, Z_AQ // 512)),
                  pl.BlockSpec((tm, 512), lambda i: (i, Z_AK // 512)),
                  pl.BlockSpec((tm, 512), lambda i: (i, Z_AV // 512)),
                  pl.BlockSpec((tm, 512), tab), pl.BlockSpec((tm, 512), tab),
                  full(1, 512), full(1, 512), full(LANES, LANES)],
        out_specs=[pl.BlockSpec((tm, 512), lambda i: (i, 0)), pl.BlockSpec((tm, 512), lambda i: (i, 0)),
                   pl.BlockSpec((A_HEADS, 1, A_DV + V_PAD, tm), lambda i: (0, i, 0, 0))],
        out_shape=[jax.ShapeDtypeStruct((rows, 512), BF16), jax.ShapeDtypeStruct((rows, 512), BF16),
                   jax.ShapeDtypeStruct((A_HEADS, rows // tm, A_DV + V_PAD, tm), BF16)],
        compiler_params=pltpu.CompilerParams(dimension_semantics=("arbitrary",)),
        name='attn_prep',
    )(z, z, z, cos_t, sin_t, qn, kn, _group_ones(A_DH))


def _attn_kernel(*refs, n_lat, lam_init):
    lam_ref, sub_ref, q_ref, kc_ref, vc_ref = refs[:5]
    if n_lat:
        kl_ref, vl_ref = refs[5:7]
        o_ref, m_scr, acc_scr, *s_scr = refs[7:]
    else:
        o_ref, m_scr, acc_scr = refs[5:]

    q = q_ref[...]
    lane = lax.broadcasted_iota(jnp.int32, (1, LANES), 1)
    zero = jnp.zeros_like(q)
    qs = (jnp.where(lane < A_DH, q, zero), jnp.where(lane >= A_DH, q, zero))
    m_scr[...] = jnp.full_like(m_scr, -jnp.inf)
    acc_scr[...] = jnp.zeros_like(acc_scr)

    def scores(kb):
        return [_dot_nt(kb, qm) for qm in qs]

    def absorb(s, smax, vt):
        m_old = [m_scr[i] for i in range(2)]
        m_new = [jnp.maximum(m_old[i], smax[i]) for i in range(2)]
        p = [jnp.exp2(s[i] - m_new[i]).astype(BF16) for i in range(2)]
        pv = [_dot(vt, p[i]) for i in range(2)]
        for i in range(2):
            acc_scr[i] = jnp.exp2(m_old[i] - m_new[i]) * acc_scr[i] + pv[i]
            m_scr[i] = m_new[i]

    colmax = lambda s: [jnp.max(a, axis=0, keepdims=True) for a in s]
    s_ctx = scores(kc_ref[...])
    absorb(s_ctx, colmax(s_ctx), vc_ref[0, 0])
    if n_lat:
        def keys(c):
            return kl_ref[pl.ds(pl.multiple_of(c * KV_TILE, KV_TILE), KV_TILE), :]

        nbuf = len(s_scr) // 2
        s_buf, max_buf = s_scr[:nbuf], s_scr[nbuf:]

        def put(slot, s):
            for i in range(2):
                s_buf[slot][i] = s[i]
                max_buf[slot][i] = jnp.max(s[i], axis=0, keepdims=True)

        ahead = nbuf - ATT_LOOKBACK
        for u in range(ahead):
            put(u, scores(keys(u)))

        def body(j, carry):
            c = nbuf * j
            for u in range(nbuf):
                put((u + ahead) % nbuf, scores(keys(jnp.minimum(c + u + ahead, n_lat - 1))))
                absorb([s_buf[u][0], s_buf[u][1]], [max_buf[u][0], max_buf[u][1]], vl_ref[0, c + u])
            return carry
        lax.fori_loop(0, n_lat // nbuf, body, 0)

    lp = lam_ref[...]
    lam = (jnp.exp(jnp.sum(lp[0:1, :] * lp[1:2, :], axis=1, keepdims=True))
           - jnp.exp(jnp.sum(lp[2:3, :] * lp[3:4, :], axis=1, keepdims=True)) + lam_init)
    a0, a1 = acc_scr[0], acc_scr[1]
    o = a0[:A_DV] / a0[A_DV:A_DV + 1] - lam * (a1[:A_DV] / a1[A_DV:A_DV + 1])
    ms = jnp.mean(o * o, axis=0, keepdims=True)
    o = o * lax.rsqrt(ms + NORM_EPS) * sub_ref[...] * (1.0 - lam_init)
    o_ref[...] = o.T


def _attention(qr, kr, vt, lam_p, subln, lam_init, geom, batch, ctx_queries):
    nl, seq, ctx = geom
    ctx_blk = nl // ctx
    per_tile = KV_TILE // ctx
    full = lambda *s: pl.BlockSpec(s, lambda b, h, i: (0,) * len(s))
    kv_ctx = [pl.BlockSpec((ctx, LANES), lambda b, h, i: (ctx_blk + b, h)),
              pl.BlockSpec((1, 1, A_DV + V_PAD, ctx),
                           lambda b, h, i: (h, nl // KV_TILE + b // per_tile, 0, lax.rem(b, per_tile)))]
    head = [full(4, A_DH), full(A_DV, 1)]
    scratch = lambda tq: [pltpu.VMEM((2, 1, tq), F32), pltpu.VMEM((2, A_DV + V_PAD, tq), F32)]
    if ctx_queries:
        tq, nq, n_lat, out_rows = ctx, 1, 0, batch * ctx
        in_specs = head + [pl.BlockSpec((tq, LANES), lambda b, h, i: (ctx_blk + b, h))] + kv_ctx
        args = (lam_p, subln.reshape(A_DV, 1), qr, kr, vt)
        omap = lambda b, h, i: (b, h)
        scratch_shapes = scratch(tq)
    else:
        tq, out_rows = Q_TILE, nl
        nq, n_lat = seq // tq, seq // KV_TILE
        assert n_lat % ATT_BUFS == 0
        omap = lambda b, h, i: (b * nq + i, h)
        in_specs = head + [pl.BlockSpec((tq, LANES), omap)] + kv_ctx + [
            pl.BlockSpec((seq, LANES), lambda b, h, i: (b, h)),
            pl.BlockSpec((1, n_lat, A_DV + V_PAD, KV_TILE), lambda b, h, i: (h, b, 0, 0))]
        args = (lam_p, subln.reshape(A_DV, 1), qr, kr, vt, kr, vt)
        scratch_shapes = (scratch(tq) + [pltpu.VMEM((2, KV_TILE, tq), F32)] * ATT_BUFS
                          + [pltpu.VMEM((2, 1, tq), F32)] * ATT_BUFS)
    return pl.pallas_call(
        functools.partial(_attn_kernel, n_lat=n_lat, lam_init=lam_init),
        grid=(batch, A_HEADS, nq),
        in_specs=in_specs,
        out_specs=pl.BlockSpec((tq, LANES), omap),
        out_shape=jax.ShapeDtypeStruct((out_rows, MIX_W), F32),
        scratch_shapes=scratch_shapes,
        compiler_params=pltpu.CompilerParams(dimension_semantics=("arbitrary", "arbitrary", "arbitrary")),
        name='diff_attn_ctx' if ctx_queries else 'diff_attn',
    )(*args)


def _merge_kernel(x_ref, mod_ref, hm_ref, mo_ref, on_ref, yr_ref, ya_ref, g_ref, wb_ref, wo_ref, o_ref):
    hm = hm_ref[0] + hm_ref[1]
    on = on_ref[...]
    parts = []
    for h in range(M_HEADS):
        blk = hm[:, h * M_DV:(h + 1) * M_DV]
        ms = jnp.mean(blk * blk, axis=-1, keepdims=True)
        parts.append(blk * lax.rsqrt(ms + NORM_EPS) * on[:, h * M_DV:(h + 1) * M_DV])
    ym = jnp.concatenate(parts, axis=1) * _sigmoid(mo_ref[...].astype(F32))
    d = x_ref.shape[1]
    zsum = _sigmoid(g_ref[:, 0:d].astype(F32)) *_dot(ym.astype(BF16), wb_ref[0])
    zsum += _sigmoid(g_ref[:, d:2 * d].astype(F32)) *_dot(yr_ref[...].astype(BF16), wb_ref[1])
    zsum += _sigmoid(g_ref[:, 2 * d:3 * d].astype(F32)) *_dot(ya_ref[...].astype(BF16), wb_ref[2])
    o_ref[...] = x_ref[...] + mod_ref[0, 5:6, :] * _dot(zsum.astype(BF16), wo_ref[...])


def _merge(xr, mods, hm, z, out_norm, yr, ya, w_branch, w_o, rows, stream_of_tile):
    d = xr.shape[1]
    tm = ROW_TILE
    full = lambda *s: pl.BlockSpec(s, lambda i: (0,) * len(s))
    return pl.pallas_call(
        _merge_kernel,
        grid=(rows // tm,),
        in_specs=[pl.BlockSpec((tm, d), lambda i: (i, 0)),
                  pl.BlockSpec((1, N_ADA, d), lambda i: (stream_of_tile(i), 0, 0)),
                  pl.BlockSpec((2, tm, MIX_W), lambda i: (0, i, 0)),
                  pl.BlockSpec((tm, MIX_W), lambda i: (i, Z_MO // MIX_W)),
                  full(1, MIX_W),
                  pl.BlockSpec((tm, MIX_W), lambda i: (i, 0)),
                  pl.BlockSpec((tm, MIX_W), lambda i: (i, 0)),
                  pl.BlockSpec((tm, 3 * d), lambda i: (i, Z_G // (3 * d))),
                  full(3, MIX_W, d), full(d, d)],
        out_specs=pl.BlockSpec((tm, d), lambda i: (i, 0)),
        out_shape=jax.ShapeDtypeStruct((rows, d), F32),
        compiler_params=pltpu.CompilerParams(dimension_semantics=("arbitrary",)),
        name='merge',
    )(xr, mods, hm, z, out_norm.reshape(1, MIX_W), yr, ya, z, w_branch, w_o)


def _lambda_init(layer):
    return 0.8 - 0.6 * math.exp(-0.3 * layer)


def kernel(x, c, ctx, c_ctx, w_ada, b_ada, norm_g, ffn1_w_gu, ffn1_w_down, ffn2_w_gu, ffn2_w_down,
           w_in, m_conv, m_gate_bias, m_out_norm, r_conv, r_w0, r_w2, r_a0, r_a2, r_g2, r_kk, r_ka,
           r_rk, r_ln_w, r_ln_b, a_qk_norm, a_lambda, a_subln, w_branch, w_o):
    batch, seq, d = x.shape
    ctx_len = ctx.shape[1]
    depth = w_ada.shape[0]
    nl, nc = batch * seq, batch * ctx_len
    rows = nl + nc
    geom = (nl, seq, ctx_len)
    tm = ROW_TILE
    assert seq % tm == 0 and nc % tm == 0 and KV_TILE == tm and KV_TILE % ctx_len == 0 and nl % ctx_len == 0
    assert ctx_len % CHUNK == 0 and seq % GRID_W == 0 and d == 1024

    tiles_per_seq = seq // tm
    stream_of_tile = lambda i: jnp.minimum(i // tiles_per_seq, batch)

    cv = jnp.zeros((SUBLANES, d), F32).at[:batch].set(c).at[batch].set(c_ctx)
    mods_all = _ada_all(cv, w_ada, b_ada)[:, :batch + 1].reshape(depth, batch + 1, N_ADA, d)

    zcols = _z_column_sources()
    zsrc = jnp.asarray(np.maximum(zcols, 0))
    zmask = jnp.asarray((zcols >= 0).astype(np.float32))
    cos_t, sin_t = _rope_tables(seq, tm)

    xr = jnp.concatenate([x.reshape(nl, d), ctx.reshape(nc, d)], axis=0)
    for li in range(depth):
        last = li == depth - 1
        mods = mods_all[li]
        w_z = (jnp.take(w_in[li], zsrc, axis=1) * zmask).astype(BF16)
        w_z = w_z.reshape(d, Z_W // Z_TILE, Z_TILE).transpose(1, 0, 2)

        xr = _ffn(xr, mods, norm_g[li, 0], ffn1_w_gu[li].astype(BF16), ffn1_w_down[li].astype(BF16),
                  0, rows, stream_of_tile)
        z = _inproj(xr, mods, norm_g[li, 1], w_z, stream_of_tile)

        qk_act, gcol = _mprep(z, m_conv[li], m_gate_bias[li], geom)
        qk_hm = qk_act.reshape(rows, 2 * M_HEADS, M_DQK).transpose(1, 0, 2)
        grow = gcol[:, :16].reshape(rows // CHUNK, CHUNK, 16).transpose(0, 2, 1)
        hm = _mlstm_scan(qk_hm, z, gcol, grow, geom, batch)

        rkv_c, g_r, vb, *scan_in = _rprep(z, r_conv[li], r_w0[li], r_w2[li], r_a0[li], r_a2[li], r_g2[li],
                                           r_kk[li], r_ka[li], geom)
        y_tm = _rwkv_scan(*scan_in[:6], vb, scan_in[6], geom, batch)
        yr = _rpost(y_tm, rkv_c, g_r, r_ln_w[li], r_ln_b[li], r_rk[li])

        qr, kr, vt = _aprep(z, cos_t, sin_t, a_qk_norm[li], geom)
        att = lambda cq: _attention(qr, kr, vt, a_lambda[li], a_subln[li], _lambda_init(li), geom, batch, cq)
        ya = att(False) if last else jnp.concatenate([att(False), att(True)], axis=0)

        out_rows = nl if last else rows
        xr = _merge(xr, mods, hm, z, m_out_norm[li], yr, ya, w_branch[li].astype(BF16),
                    w_o[li].astype(BF16), out_rows, stream_of_tile)
        xr = _ffn(xr, mods, norm_g[li, 2], ffn2_w_gu[li].astype(BF16), ffn2_w_down[li].astype(BF16),
                  6, out_rows, stream_of_tile)
    return xr[:nl].reshape(batch, seq, d)
```

```python
import functools
import math

import numpy as np
import jax
import jax.numpy as jnp
from jax import lax
from jax.experimental import pallas as pl
from jax.experimental.pallas import tpu as pltpu

F32 = jnp.float32
BF16 = jnp.bfloat16
HI = lax.Precision.HIGHEST

N_ADA = 9
MACARON_W = 0.5
NORM_EPS = 1e-6
MIX_W = 512
GRID_W = 64
CHUNK = 64
SCAN_CHUNKS = 4
M_SCAN_CHUNKS = 2

M_HEADS = 4
M_DQK = 64
M_DV = 128
R_HEADS = 8
R_DH = 64
R_GN_EPS = 64e-5
A_HEADS = 4
A_DH = 64
A_DV = 128
ROPE_BASE = 10000.0

LANES = 128
SUBLANES = 8
HALO_ROWS = 16
ROW_TILE = 512
RWKV_PREP_TILE = 256
Q_TILE = 512
KV_TILE = 512
V_PAD = 16
LOG2_E = math.log2(math.e)
ATT_BUFS = 4
ATT_UNROLL = 16
FFN_CHUNK = 256
FFN_VMEM_LIMIT = 48 * 2 ** 20

_IN_SPLITS = (
    ('m_q', 256), ('m_k', 256), ('m_v', 512), ('m_o', 512),
    ('m_if', 4), ('m_ff', 4), ('m_ib', 4), ('m_fb', 4),
    ('r_r', 512), ('r_k', 512), ('r_v', 512),
    ('r_wf', 64), ('r_wb', 64), ('r_af', 64), ('r_ab', 64), ('r_g', 128),
    ('a_q', 512), ('a_k', 512), ('a_v', 512),
    ('g_m', 1024), ('g_r', 1024), ('g_a', 1024),
)
Z_G, Z_MQK, Z_MV, Z_MO, Z_RKV, Z_AQ, Z_AK, Z_AV, Z_LORA, Z_MG, Z_W = (
    0, 3072, 3584, 4096, 4608, 6144, 6656, 7168, 7680, 8064, 8192)
Z_TILE = 2048


def _deinterleave64():
    return np.concatenate([np.arange(0, 64, 2), np.arange(1, 64, 2)])


def _z_column_sources():
    off, start = {}, 0
    for name, w in _IN_SPLITS:
        off[name] = start
        start += w
    cols = []
    rng = lambda n: list(range(off[n], off[n] + dict(_IN_SPLITS)[n]))
    cols += rng('g_m') + rng('g_r') + rng('g_a')
    cols += rng('m_q') + rng('m_k') + rng('m_v') + rng('m_o')
    cols += rng('r_r') + rng('r_k') + rng('r_v')
    perm = _deinterleave64()
    for n in ('a_q', 'a_k'):
        for g in range(8):
            cols += list(off[n] + g * 64 + perm)
    cols += rng('a_v')
    cols += rng('r_wf') + rng('r_wb') + rng('r_af') + rng('r_ab') + rng('r_g')
    cols += rng('m_if') + rng('m_ff') + rng('m_ib') + rng('m_fb')
    cols += [-1] * (Z_W - len(cols))
    assert len(cols) == Z_W
    return np.asarray(cols, np.int32)


def _sigmoid(x):
    return 1.0 / (1.0 + jnp.exp(-x))


def _dot(a, b, prec=None):
    return jnp.dot(a, b, preferred_element_type=F32, precision=prec)


def _dot_nt(a, b, prec=None):
    return lax.dot_general(a, b, (((1,), (1,)), ((), ())), preferred_element_type=F32, precision=prec)


def _dot_tn(a, b, prec=None):
    return lax.dot_general(a, b, (((0,), (0,)), ((), ())), preferred_element_type=F32, precision=prec)


def _norm_mod(x, g, shift, scale):
    ms = jnp.mean(x * x, axis=-1, keepdims=True)
    y = x * lax.rsqrt(ms + NORM_EPS) * g
    return y * (1.0 + scale) + shift


def _ada_kernel(c_ref, w_ref, b_ref, o_ref):
    c = c_ref[...]
    s = c * _sigmoid(c)
    o_ref[0] = _dot(s, w_ref[0], HI) + b_ref[0]


def _ada_all(cv, w_ada, b_ada):
    depth, d, nd = w_ada.shape
    tn = 1024
    return pl.pallas_call(
        _ada_kernel,
        grid=(depth, nd // tn),
        in_specs=[pl.BlockSpec((SUBLANES, d), lambda l, n: (0, 0)),
                  pl.BlockSpec((1, d, tn), lambda l, n: (l, 0, n)),
                  pl.BlockSpec((1, 1, tn), lambda l, n: (l, 0, n))],
        out_specs=pl.BlockSpec((1, SUBLANES, tn), lambda l, n: (l, 0, n)),
        out_shape=jax.ShapeDtypeStruct((depth, SUBLANES, nd), F32),
        name='ada',
    )(cv, w_ada, b_ada.reshape(depth, 1, nd))


def _ffn_kernel(x_ref, mod_ref, g_ref, wgu_ref, wd_ref, o_ref, u_scr, *, mi):
    dff = wd_ref.shape[0]
    h = _norm_mod(x_ref[...], g_ref[...], mod_ref[0, mi:mi + 1, :], mod_ref[0, mi + 1:mi + 2, :]).astype(BF16)
    for j in range(dff // FFN_CHUNK):
        lo, hi = j * FFN_CHUNK, (j + 1) * FFN_CHUNK
        a = _dot(h, wgu_ref[:, lo:hi])
        b = _dot(h, wgu_ref[:, dff + lo:dff + hi])
        u_scr[:, lo:hi] = ((a * _sigmoid(a)) * b).astype(BF16)
    o_ref[...] = x_ref[...] + MACARON_W * mod_ref[0, mi + 2:mi + 3, :] * _dot(u_scr[...], wd_ref[...])


def _ffn(xr, mods, g, w_gu, w_down, mi, rows, stream_of_tile):
    d = xr.shape[1]
    dff = w_down.shape[0]
    assert dff % FFN_CHUNK == 0
    tm = ROW_TILE
    resident = lambda *s: pl.BlockSpec(s, lambda i: (0,) * len(s), pipeline_mode=pl.Buffered(1))
    return pl.pallas_call(
        functools.partial(_ffn_kernel, mi=mi),
        grid=(rows // tm,),
        in_specs=[pl.BlockSpec((tm, d), lambda i: (i, 0)),
                  pl.BlockSpec((1, N_ADA, d), lambda i: (stream_of_tile(i), 0, 0)),
                  pl.BlockSpec((1, d), lambda i: (0, 0)),
                  resident(d, 2 * dff), resident(dff, d)],
        out_specs=pl.BlockSpec((tm, d), lambda i: (i, 0)),
        out_shape=jax.ShapeDtypeStruct((rows, d), F32),
        scratch_shapes=[pltpu.VMEM((tm, dff), BF16)],
        compiler_params=pltpu.CompilerParams(dimension_semantics=("arbitrary",),
                                             vmem_limit_bytes=FFN_VMEM_LIMIT),
        name='ffn',
    )(xr, mods, g.reshape(1, d), w_gu, w_down)


def _inproj_kernel(x_ref, mod_ref, g_ref, w_ref, o_ref, h_scr):
    @pl.when(pl.program_id(1) == 0)
    def _():
        h = _norm_mod(x_ref[...], g_ref[...], mod_ref[0, 3:4, :], mod_ref[0, 4:5, :])
        h_scr[...] = h.astype(BF16)

    o_ref[...] = _dot(h_scr[...], w_ref[pl.program_id(1)]).astype(o_ref.dtype)


def _inproj(xr, mods, g, w_z, stream_of_tile):
    rows, d = xr.shape
    tm = ROW_TILE
    nz = Z_W // Z_TILE
    return pl.pallas_call(
        _inproj_kernel,
        grid=(rows // tm, nz),
        in_specs=[pl.BlockSpec((tm, d), lambda i, n: (i, 0)),
                  pl.BlockSpec((1, N_ADA, d), lambda i, n: (stream_of_tile(i), 0, 0)),
                  pl.BlockSpec((1, d), lambda i, n: (0, 0)),
                  pl.BlockSpec((nz, d, Z_TILE), lambda i, n: (0, 0, 0), pipeline_mode=pl.Buffered(1))],
        out_specs=pl.BlockSpec((tm, Z_TILE), lambda i, n: (i, n)),
        out_shape=jax.ShapeDtypeStruct((rows, Z_W), BF16),
        scratch_shapes=[pltpu.VMEM((tm, d), BF16)],
        compiler_params=pltpu.CompilerParams(dimension_semantics=("arbitrary", "arbitrary"),
                                             vmem_limit_bytes=FFN_VMEM_LIMIT),
        name='inproj',
    )(xr, mods, g.reshape(1, d), w_z)


def _conv3(x, prev_rows, next_rows, w, row0, geom):
    nl, seq, ctx = geom
    tm = x.shape[0]
    x = x.astype(F32)
    t = lax.broadcasted_iota(jnp.int32, (tm, 1), 0)
    r = row0 + t
    is_lat = r < nl
    pos = jnp.where(is_lat, lax.rem(r, seq), lax.rem(jnp.maximum(r - nl, 0), ctx))
    seglen = jnp.where(is_lat, seq, ctx)
    xm = pltpu.roll(x, 1, 0)
    xm = jnp.where(t == 0, prev_rows[HALO_ROWS - 1:HALO_ROWS, :].astype(F32), xm)
    xm = jnp.where(pos == 0, 0.0, xm)
    xp = pltpu.roll(x, tm - 1, 0)
    xp = jnp.where(t == tm - 1, next_rows[0:1, :].astype(F32), xp)
    xp = jnp.where(pos == seglen - 1, 0.0, xp)
    return xm * w[0:1, :] + x * w[1:2, :] + xp * w[2:3, :]


def _halo_specs(tm, width, col_blk, rows):
    per = tm // HALO_ROWS
    last = rows // HALO_ROWS - 1
    return [pl.BlockSpec((tm, width), lambda i: (i, col_blk)),
            pl.BlockSpec((HALO_ROWS, width), lambda i: (jnp.maximum(i * per - 1, 0), col_blk)),
            pl.BlockSpec((HALO_ROWS, width), lambda i: (jnp.minimum((i + 1) * per, last), col_blk))]


def _mprep_kernel(x_ref, xp_ref, xn_ref, w_ref, gt_ref, gb_ref, qk_ref, go_ref, *, geom):
    tm = x_ref.shape[0]
    y = _conv3(x_ref[...], xp_ref[...], xn_ref[...], w_ref[...], pl.program_id(0) * tm, geom)
    y = y * _sigmoid(y)
    lane = lax.broadcasted_iota(jnp.int32, (1, y.shape[1]), 1)
    qk_ref[...] = jnp.where(lane >= M_HEADS * M_DQK, y * (M_DQK ** -0.5), y)
    g = gt_ref[...].astype(F32) + gb_ref[...]
    gl = lax.broadcasted_iota(jnp.int32, (1, LANES), 1)
    is_forget = (lax.rem(gl, 2 * M_HEADS) >= M_HEADS) & (gl < 4 * M_HEADS)
    logsig = jnp.minimum(g, 0.0) - jnp.log(1.0 + jnp.exp(-jnp.abs(g)))
    go_ref[...] = jnp.where(is_forget, logsig, g)


def _mprep(z, conv_w, gate_bias, geom):
    rows = z.shape[0]
    tm = ROW_TILE
    gb = jnp.zeros((1, LANES), F32).at[0, :4 * M_HEADS].set(gate_bias.reshape(-1))
    return pl.pallas_call(
        functools.partial(_mprep_kernel, geom=geom),
        grid=(rows // tm,),
        in_specs=_halo_specs(tm, 512, Z_MQK // 512, rows) + [
            pl.BlockSpec((3, 512), lambda i: (0, 0)),
            pl.BlockSpec((tm, LANES), lambda i: (i, Z_MG // LANES)),
            pl.BlockSpec((1, LANES), lambda i: (0, 0))],
        out_specs=[pl.BlockSpec((tm, 512), lambda i: (i, 0)),
                   pl.BlockSpec((tm, LANES), lambda i: (i, 0))],
        out_shape=[jax.ShapeDtypeStruct((rows, 512), F32), jax.ShapeDtypeStruct((rows, LANES), F32)],
        compiler_params=pltpu.CompilerParams(dimension_semantics=("arbitrary",)),
        name='mlstm_prep',
    )(z, z, z, conv_w, z, gb)


def _chunk_block(d, b, j, geom, size=CHUNK):
    nl, seq, ctx = geom
    nc_c, nc_l = ctx // size, seq // size
    jl = j - nc_c
    c_ctx = jnp.where(d == 0, j, nc_c - 1 - j)
    c_lat = jnp.where(d == 0, jl, nc_l - 1 - jl)
    return jnp.where(j < nc_c, nl // size + b * nc_c + c_ctx, b * nc_l + c_lat)


def _order_masks(d):
    ti = lax.broadcasted_iota(jnp.int32, (CHUNK, CHUNK), 0)
    si = lax.broadcasted_iota(jnp.int32, (CHUNK, CHUNK), 1)
    fwd = d == 0
    ahead = jnp.where(fwd, si - ti, ti - si)
    incl = ahead <= 0
    strict = ahead < 0
    incl_t = ahead >= 0
    return fwd, incl, strict, incl_t, ti == si


def _mlstm_kernel(qk_ref, v_ref, gc_ref, gr_ref, o_ref, ct_scr, m_scr):
    d = pl.program_id(0)

    @pl.when(pl.program_id(2) == 0)
    def _():
        ct_scr[...] = jnp.zeros_like(ct_scr)
        m_scr[...] = jnp.zeros_like(m_scr)

    fwd, incl, _, incl_t, _ = _order_masks(d)
    heads = range(M_HEADS)
    chunks = range(M_SCAN_CHUNKS)
    lane = lax.broadcasted_iota(jnp.int32, (CHUNK, LANES), 1)
    ones_col = jnp.where(lane == 0, 1.0, 0.0)
    cat1 = lambda xs: jnp.concatenate(xs, axis=1)

    pre = []
    for u in chunks:
        cu = jnp.where(fwd, u, M_SCAN_CHUNKS - 1 - u)
        rows = pl.ds(pl.multiple_of(cu * CHUNK, CHUNK), CHUNK)
        gc = gc_ref[rows, :]
        gr = gr_ref[cu]
        q = [qk_ref[h, rows, :].astype(BF16) for h in heads]
        k = [qk_ref[M_HEADS + h, rows, :].astype(BF16) for h in heads]
        v = [v_ref[rows, h * M_DV:(h + 1) * M_DV].astype(F32) for h in heads]
        qk = [_dot_nt(q[h], k[h]) for h in heads]
        per_head = []
        for h in heads:
            i_col = jnp.where(fwd, gc[:, h:h + 1], gc[:, 8 + h:9 + h])
            f_col = jnp.where(fwd, gc[:, 4 + h:5 + h], gc[:, 12 + h:13 + h])
            i_row = jnp.where(fwd, gr[h:h + 1, :], gr[8 + h:9 + h, :])
            f_row = jnp.where(fwd, gr[4 + h:5 + h, :], gr[12 + h:13 + h, :])
            bcum_col = jnp.sum(jnp.where(incl, f_row, 0.0), axis=1, keepdims=True)
            bcum_row = jnp.sum(jnp.where(incl_t, f_col, 0.0), axis=0, keepdims=True)
            dlog = jnp.where(incl, bcum_col - bcum_row + i_row, -jnp.inf)
            rmax = jnp.max(dlog, axis=1, keepdims=True)
            btot = jnp.sum(f_col, axis=0, keepdims=True)
            wlog = btot - bcum_col + i_col
            per_head.append(dict(bcum=bcum_col, rmax=rmax, btot=btot, wlog=wlog,
                                 wlog_row=btot - bcum_row + i_row,
                                 wmax=jnp.max(wlog, axis=0, keepdims=True),
                                 qkd=(qk[h] * jnp.exp(dlog - rmax)).astype(BF16)))
        v_aug = [cat1([v[h], ones_col]) for h in heads]
        v_aug_b = [a.astype(BF16) for a in v_aug]
        sv0 = [_dot(per_head[h]['qkd'], v_aug_b[h]) for h in heads]
        cross = [[_dot_nt(q[h], p['k'][h]) for h in heads] for p in pre]
        pre.append(dict(rows=rows, q=q, k=k, v_aug=v_aug, v_aug_b=v_aug_b, sv0=sv0, cross=cross, g=per_head))

    c0 = [ct_scr[h] for h in heads]
    c0b = [a.astype(BF16) for a in c0]
    m_st = [m_scr[h] for h in heads]
    ct = list(c0)
    decay = [1.0 for _ in heads]
    earlier = []
    for u in chunks:
        c = pre[u]
        qc = [decay[h] * _dot(c['q'][h], c0b[h]) for h in heads]
        for e, w_rows in enumerate(earlier):
            for h in heads:
                qc[h] = qc[h] + _dot((c['cross'][e][h] * w_rows[h]).astype(BF16), pre[e]['v_aug_b'][h])
        m_new = [jnp.maximum(c['g'][h]['btot'] + m_st[h], c['g'][h]['wmax']) for h in heads]
        dec = [jnp.exp(c['g'][h]['btot'] + m_st[h] - m_new[h]) for h in heads]
        ws = [jnp.exp(c['g'][h]['wlog'] - m_new[h]) for h in heads]
        kv = [_dot_tn(c['k'][h], (ws[h] * c['v_aug'][h]).astype(BF16)) for h in heads]
        for h in heads:
            g = c['g'][h]
            inter = g['bcum'] + m_st[h]
            mt = jnp.maximum(inter, g['rmax'])
            nd = jnp.exp(g['rmax'] - mt) * c['sv0'][h] + jnp.exp(inter - mt) * qc[h]
            den = jnp.maximum(jnp.abs(nd[:, M_DV:M_DV + 1]), jnp.exp(-mt))
            o_ref[0, c['rows'], h * M_DV:(h + 1) * M_DV] = nd[:, :M_DV] / den
            ct[h] = dec[h] * ct[h] + kv[h]
        earlier = [[w_rows[h] * dec[h] for h in heads] for w_rows in earlier]
        earlier.append([jnp.exp(c['g'][h]['wlog_row'] - m_new[h]) for h in heads])
        decay = [decay[h] * dec[h] for h in heads]
        m_st = m_new
    for h in heads:
        ct_scr[h] = ct[h]
        m_scr[h] = m_st[h]


def _mlstm_scan(qk_hm, z, gcol, grow, geom, batch):
    rows = z.shape[0]
    step = M_SCAN_CHUNKS * CHUNK
    assert geom[1] % step == 0 and geom[2] % step == 0
    nch = (geom[1] + geom[2]) // step
    blk = lambda d, b, j: _chunk_block(d, b, j, geom, step)
    return pl.pallas_call(
        _mlstm_kernel,
        grid=(2, batch, nch),
        in_specs=[pl.BlockSpec((2 * M_HEADS, step, M_DQK), lambda d, b, j: (0, blk(d, b, j), 0)),
                  pl.BlockSpec((step, MIX_W), lambda d, b, j: (blk(d, b, j), Z_MV // MIX_W)),
                  pl.BlockSpec((step, LANES), lambda d, b, j: (blk(d, b, j), 0)),
                  pl.BlockSpec((M_SCAN_CHUNKS, 16, CHUNK), lambda d, b, j: (blk(d, b, j), 0, 0))],
        out_specs=pl.BlockSpec((1, step, MIX_W), lambda d, b, j: (d, blk(d, b, j), 0)),
        out_shape=jax.ShapeDtypeStruct((2, rows, MIX_W), F32),
        scratch_shapes=[pltpu.VMEM((M_HEADS, M_DQK, M_DV + LANES), F32),
                        pltpu.VMEM((M_HEADS, 1, 1), F32)],
        compiler_params=pltpu.CompilerParams(dimension_semantics=("arbitrary", "arbitrary", "arbitrary")),
        name='mlstm_scan',
    )(qk_hm, z, gcol, grow)


def _rprep_kernel(x_ref, xp_ref, xn_ref, cw_ref, lo_ref, w0_ref, w2_ref, a0_ref, a2_ref, g2_ref,
                  kkw_ref, kaw_ref, bd_ref, tri_ref,
                  rkv_ref, g_ref, vb_ref, at_ref, bt_ref, kt_ref, rt_ref, bb_ref, kb_ref, et_ref, *, geom):
    tm = x_ref.shape[0]
    rkv = _conv3(x_ref[...], xp_ref[...], xn_ref[...], cw_ref[...], pl.program_id(0) * tm, geom)
    rkv_ref[...] = rkv
    r = rkv[:, 0:MIX_W]
    k = rkv[:, MIX_W:2 * MIX_W]
    vb_ref[...] = rkv[:, 2 * MIX_W:3 * MIX_W].astype(BF16)
    lo = lo_ref[...].astype(F32)
    g_ref[...] = _dot(_sigmoid(lo[:, 256:384]), g2_ref[...], HI)
    kkr = k * kkw_ref[...]
    kk = kkr / jnp.maximum(jnp.sqrt(_group_sum(kkr * kkr, bd_ref[...])), 1e-12)
    for d in range(2):
        w_raw = w0_ref[d:d + 1, :] + _dot(jnp.tanh(lo[:, 64 * d:64 * d + 64]), w2_ref[d], HI)
        lw = -_sigmoid(w_raw) * math.exp(-0.5)
        a = _sigmoid(a0_ref[d:d + 1, :] + _dot(lo[:, 128 + 64 * d:192 + 64 * d], a2_ref[d], HI))
        hi = lw.astype(BF16)
        rem = lw - hi.astype(F32)
        mid = rem.astype(BF16)
        low = (rem - mid.astype(F32)).astype(BF16)
        parts = jnp.concatenate([hi, mid, low], axis=1)
        c3 = _dot(tri_ref[d, 0], parts)
        s3 = _dot(tri_ref[d, 1], parts)
        cum = c3[:, 0:MIX_W] + c3[:, MIX_W:2 * MIX_W] + c3[:, 2 * MIX_W:3 * MIX_W]
        suf = s3[:, 0:MIX_W] + s3[:, MIX_W:2 * MIX_W] + s3[:, 2 * MIX_W:3 * MIX_W]
        kd = k * (1.0 + (a - 1.0) * kaw_ref[...])
        kka = kk * a
        e_neg = jnp.exp(-cum)
        e_end = jnp.exp(suf)
        at_ref[d] = (-kk * jnp.exp(cum - lw)).astype(BF16)
        bt_ref[d] = (kka * e_neg).astype(BF16)
        kt_ref[d] = (kd * e_neg).astype(BF16)
        rt_ref[d] = (r * jnp.exp(cum)).astype(BF16)
        bb_ref[d] = (kka * e_end).astype(BF16)
        kb_ref[d] = (kd * e_end).astype(BF16)
        et_ref[d] = jnp.exp(cum + suf)


def _chunk_order_matrices(tm):
    i = np.arange(tm)
    same = (i[:, None] // CHUNK) == (i[None, :] // CHUNK)
    le = i[None, :] <= i[:, None]
    ge = i[None, :] >= i[:, None]
    mats = np.stack([np.stack([same & le, same & ~le]), np.stack([same & ge, same & ~ge])])
    return jnp.asarray(mats.astype(np.float32), dtype=BF16)


def _rprep(z, conv_w, w0, w2, a0, a2, g2, kk_w, ka_w, geom):
    rows = z.shape[0]
    tm = RWKV_PREP_TILE
    full = lambda *s: pl.BlockSpec(s, lambda i: (0,) * len(s))
    row = lambda w: pl.BlockSpec((tm, w), lambda i: (i, 0))
    both = pl.BlockSpec((2, tm, MIX_W), lambda i: (0, i, 0))
    shp = lambda dt: jax.ShapeDtypeStruct((2, rows, MIX_W), dt)
    return pl.pallas_call(
        functools.partial(_rprep_kernel, geom=geom),
        grid=(rows // tm,),
        in_specs=_halo_specs(tm, 1536, Z_RKV // 1536, rows) + [
            full(3, 1536),
            pl.BlockSpec((tm, 384), lambda i: (i, Z_LORA // 384)),
            full(2, 512), full(2, 64, 512), full(2, 512), full(2, 64, 512), full(128, 512),
            full(1, MIX_W), full(1, MIX_W), full(LANES, LANES), full(2, 2, tm, tm)],
        out_specs=[row(1536), row(MIX_W), row(MIX_W)] + [both] * 7,
        out_shape=[jax.ShapeDtypeStruct((rows, 1536), F32), jax.ShapeDtypeStruct((rows, MIX_W), F32),
                   jax.ShapeDtypeStruct((rows, MIX_W), BF16)] + [shp(BF16)] * 6 + [shp(F32)],
        compiler_params=pltpu.CompilerParams(dimension_semantics=("arbitrary",)),
        name='rwkv_prep',
    )(z, z, z, conv_w, z, w0, w2, a0, a2, g2, kk_w.reshape(1, MIX_W), ka_w.reshape(1, MIX_W),
      _group_ones(R_DH), _chunk_order_matrices(tm))


def _rwkv_kernel(at_ref, bt_ref, kt_ref, rt_ref, bb_ref, kb_ref, v_ref, et_ref, y_ref, ss_scr):
    d = pl.program_id(0)

    @pl.when(pl.program_id(2) == 0)
    def _():
        ss_scr[...] = jnp.zeros_like(ss_scr)

    ti = lax.broadcasted_iota(jnp.int32, (CHUNK, LANES), 0)
    li = lax.broadcasted_iota(jnp.int32, (CHUNK, LANES), 1)
    si = jnp.bitwise_and(li, R_DH - 1)
    ahead = jnp.where(d == 0, si - ti, ti - si)
    strict2 = ahead < 0
    incl2 = ahead <= 0
    first = li < R_DH
    eye64 = (lax.broadcasted_iota(jnp.int32, (CHUNK, R_DH), 0)
             == lax.broadcasted_iota(jnp.int32, (CHUNK, R_DH), 1)).astype(F32)
    rr = lax.broadcasted_iota(jnp.int32, (LANES, LANES), 0)
    cc = lax.broadcasted_iota(jnp.int32, (LANES, LANES), 1)
    same_head = (rr < R_DH) == (cc < R_DH)
    eye128 = rr == cc
    zero = jnp.zeros((CHUNK, LANES), BF16)
    cat0 = lambda xs: jnp.concatenate(xs, axis=0)
    cat1 = lambda xs: jnp.concatenate(xs, axis=1)
    keep = lambda h, x: jnp.where(first, x, zero) if h == 0 else jnp.where(first, zero, x)

    pairs = range(R_HEADS // 2)
    items = [(u, p) for u in range(SCAN_CHUNKS) for p in pairs]
    rows_of = [pl.ds(pl.multiple_of(jnp.where(d == 0, u, SCAN_CHUNKS - 1 - u) * CHUNK, CHUNK), CHUNK)
               for u in range(SCAN_CHUNKS)]
    lanes_of = [slice(p * LANES, (p + 1) * LANES) for p in pairs]
    at = [at_ref[0, rows_of[u], lanes_of[p]] for u, p in items]
    rt = [rt_ref[0, rows_of[u], lanes_of[p]] for u, p in items]
    v = [v_ref[rows_of[u], lanes_of[p]] for u, p in items]
    n = range(len(items))
    g = [_dot_nt(cat0([keep(0, at[i]), keep(0, rt[i]), keep(1, at[i]), keep(1, rt[i])]),
                 cat0([bt_ref[0, rows_of[u], lanes_of[p]], kt_ref[0, rows_of[u], lanes_of[p]]]))
         for i, (u, p) in enumerate(items)]
    ga = [[jnp.where(strict2, g[i][2 * h * CHUNK:(2 * h + 1) * CHUNK], 0.0) for h in range(2)] for i in n]
    gr = [[jnp.where(incl2, g[i][(2 * h + 1) * CHUNK:(2 * h + 2) * CHUNK], 0.0) for h in range(2)] for i in n]
    aakv = [_dot(cat1(ga[i]).astype(BF16), cat0([zero, keep(0, v[i]), zero, keep(1, v[i])])).astype(BF16)
            for i in n]
    pw = [ga[i][h][:, :R_DH] for i in n for h in range(2)]
    tinv = [eye64 + a for a in pw]
    for _ in range(int(math.log2(CHUNK)) - 1):
        pwb = [a.astype(BF16) for a in pw]
        pw = [_dot(a, a) for a in pwb]
        tinv = [t + _dot(t.astype(BF16), a.astype(BF16)) for t, a in zip(tinv, pw)]
    tinv = [t.astype(BF16) for t in tinv]
    x = [(_dot(tinv[2 * i], cat1([keep(0, at[i]), keep(0, aakv[i])]))
          + _dot(tinv[2 * i + 1], cat1([keep(1, at[i]), keep(1, aakv[i])]))).astype(BF16) for i in n]
    y4 = [_dot(cat1(gr[i]).astype(BF16),
               cat0([cat1([keep(0, x[i][:, :LANES]), keep(0, x[i][:, LANES:])]), cat1([zero, keep(0, v[i])]),
                     cat1([keep(1, x[i][:, :LANES]), keep(1, x[i][:, LANES:])]), cat1([zero, keep(1, v[i])])]))
          for i in n]
    mn = [_dot_tn(cat0([bb_ref[0, rows_of[u], lanes_of[p]], kb_ref[0, rows_of[u], lanes_of[p]]]),
                  cat0([x[i], cat1([zero, v[i]])])) for i, (u, p) in enumerate(items)]
    lhs = []
    for i, (u, p) in enumerate(items):
        rh = rt[i].astype(F32) + y4[i][:, :LANES]
        decay = et_ref[0, pl.ds(jnp.where(d == 0, u, SCAN_CHUNKS - 1 - u) * CHUNK, 1), lanes_of[p]]
        mt = jnp.where(eye128, decay, 0.0) + jnp.where(same_head, mn[i][:, :LANES], 0.0)
        lhs.append(cat0([rh, mt]).astype(BF16))
    ss = [ss_scr[p] for p in pairs]
    for u in range(SCAN_CHUNKS):
        out = [_dot(lhs[u * len(pairs) + p], ss[p].astype(BF16)) for p in pairs]
        for p in pairs:
            i = u * len(pairs) + p
            y_ref[0, rows_of[u], lanes_of[p]] = out[p][:CHUNK] + y4[i][:, LANES:]
            ss[p] = out[p][CHUNK:] + jnp.where(same_head, mn[i][:, LANES:], 0.0)
    for p in pairs:
        ss_scr[p] = ss[p]


def _rwkv_scan(at, bt, kt, rt, bb, kb, vb, et, geom, batch):
    rows = vb.shape[0]
    step = SCAN_CHUNKS * CHUNK
    assert geom[1] % step == 0 and geom[2] % step == 0
    nch = (geom[1] + geom[2]) // step
    blk = lambda d, b, j: _chunk_block(d, b, j, geom, step)
    per_dir = pl.BlockSpec((1, step, MIX_W), lambda d, b, j: (d, blk(d, b, j), 0))
    return pl.pallas_call(
        _rwkv_kernel,
        grid=(2, batch, nch),
        in_specs=[per_dir] * 6 + [pl.BlockSpec((step, MIX_W), lambda d, b, j: (blk(d, b, j), 0)), per_dir],
        out_specs=per_dir,
        out_shape=jax.ShapeDtypeStruct((2, rows, MIX_W), F32),
        scratch_shapes=[pltpu.VMEM((R_HEADS // 2, LANES, LANES), F32)],
        compiler_params=pltpu.CompilerParams(dimension_semantics=("arbitrary", "arbitrary", "arbitrary")),
        name='rwkv_scan',
    )(at, bt, kt, rt, bb, kb, vb, et)


def _rpost_kernel(y_ref, rkv_ref, g_ref, lnw_ref, lnb_ref, rk_ref, bd_ref, o_ref):
    y = y_ref[0] + y_ref[1]
    bd = bd_ref[...]
    inv = 1.0 / R_DH
    mu = _group_sum(y, bd) * inv
    yc = y - mu
    var = _group_sum(yc * yc, bd) * inv
    yn = yc * lax.rsqrt(var + R_GN_EPS) * lnw_ref[...] + lnb_ref[...]
    r = rkv_ref[:, 0:512]
    k = rkv_ref[:, 512:1024]
    v = rkv_ref[:, 1024:1536]
    bonus = _group_sum(r * k * rk_ref[...], bd) * v
    o_ref[...] = (yn + bonus) * g_ref[...]


def _group_ones(group):
    i = np.arange(LANES) // group
    return jnp.asarray((i[:, None] == i[None, :]).astype(np.float32), dtype=BF16)


def _group_sum(x, ones):
    tm, w = x.shape
    nt = w // LANES
    xs = jnp.concatenate([x[:, t * LANES:(t + 1) * LANES] for t in range(nt)], axis=0)
    hi = xs.astype(BF16)
    rem = xs - hi.astype(F32)
    mid = rem.astype(BF16)
    low = (rem - mid.astype(F32)).astype(BF16)
    y = _dot(jnp.concatenate([hi, mid, low], axis=0), ones)
    n = nt * tm
    ys = y[0:n] + y[n:2 * n] + y[2 * n:3 * n]
    return jnp.concatenate([ys[t * tm:(t + 1) * tm] for t in range(nt)], axis=1)


def _rpost(y_tm, rkv_c, g, ln_w, ln_b, r_k):
    rows = rkv_c.shape[0]
    tm = ROW_TILE
    full = lambda *s: pl.BlockSpec(s, lambda i: (0,) * len(s))
    return pl.pallas_call(
        _rpost_kernel,
        grid=(rows // tm,),
        in_specs=[pl.BlockSpec((2, tm, 512), lambda i: (0, i, 0)),
                  pl.BlockSpec((tm, 1536), lambda i: (i, 0)),
                  pl.BlockSpec((tm, 512), lambda i: (i, 0)),
                  full(1, 512), full(1, 512), full(1, 512), full(LANES, LANES)],
        out_specs=pl.BlockSpec((tm, 512), lambda i: (i, 0)),
        out_shape=jax.ShapeDtypeStruct((rows, 512), F32),
        compiler_params=pltpu.CompilerParams(dimension_semantics=("arbitrary",)),
        name='rwkv_post',
    )(y_tm, rkv_c, g, ln_w.reshape(1, 512), ln_b.reshape(1, 512), r_k.reshape(1, 512),
      _group_ones(R_DH))


def _aprep_kernel(q_ref, k_ref, v_ref, cos_ref, sin_ref, qn_ref, kn_ref, bd_ref, qo_ref, ko_ref, vt_ref):
    tm = v_ref.shape[0]
    pad_row = lax.broadcasted_iota(jnp.int32, (V_PAD, tm), 0)
    pad = jnp.where(pad_row == 0, 1.0, 0.0).astype(BF16)
    for h in range(A_HEADS):
        vt_ref[h, 0, 0:A_DV, :] = v_ref[:, h * A_DV:(h + 1) * A_DV].astype(F32).T.astype(BF16)
        vt_ref[h, 0, A_DV:A_DV + V_PAD, :] = pad
    bd = bd_ref[...]
    cos = cos_ref[...]
    sin = sin_ref[...]
    lane = lax.broadcasted_iota(jnp.int32, (1, 512), 1)
    lower = lax.rem(lane, A_DH) < A_DH // 2

    def one(x, g):
        ms = _group_sum(x * x, bd) * (1.0 / A_DH)
        y = x * lax.rsqrt(ms + NORM_EPS) * g
        swapped = jnp.where(lower, pltpu.roll(y, 512 - A_DH // 2, 1), pltpu.roll(y, A_DH // 2, 1))
        return y * cos + swapped * sin

    qo_ref[...] = one(q_ref[...].astype(F32), qn_ref[...]).astype(BF16)
    ko_ref[...] = one(k_ref[...].astype(F32), kn_ref[...]).astype(BF16)


def _rope_tables(seq, tm):
    rows = seq // GRID_W
    row = jnp.repeat(jnp.arange(rows, dtype=F32), GRID_W)
    col = jnp.tile(jnp.arange(GRID_W, dtype=F32), rows)
    half = A_DH // 2
    inv = ROPE_BASE ** (-jnp.arange(0, half, 2, dtype=F32) / half)
    ang = jnp.concatenate([row[:, None] * inv, col[:, None] * inv], axis=-1)
    cos, sin = jnp.cos(ang), jnp.sin(ang)
    cos64 = jnp.concatenate([cos, cos], axis=-1)
    sin64 = jnp.concatenate([-sin, sin], axis=-1)
    cos_t = jnp.concatenate([jnp.tile(cos64, (1, 8)), jnp.ones((tm, 512), F32)], axis=0)
    sin_t = jnp.concatenate([jnp.tile(sin64, (1, 8)), jnp.zeros((tm, 512), F32)], axis=0)
    return cos_t, sin_t


def _aprep(z, cos_t, sin_t, qk_norm, geom):
    rows = z.shape[0]
    nl, seq, _ = geom
    tm = ROW_TILE
    perm = _deinterleave64()
    qn = jnp.tile(qk_norm[0][perm] * (A_DH ** -0.5 * LOG2_E), 8).reshape(1, 512)
    kn = jnp.tile(qk_norm[1][perm], 8).reshape(1, 512)
    tiles_per_seq = seq // tm
    tab = lambda i: (jnp.where(i < nl // tm, lax.rem(i, tiles_per_seq), tiles_per_seq), 0)
    full = lambda *s: pl.BlockSpec(s, lambda i: (0,) * len(s))
    return pl.pallas_call(
        _aprep_kernel,
        grid=(rows // tm,),
        in_specs=[pl.BlockSpec((tm, 512), lambda i: (i, Z_AQ // 512)),
                  pl.BlockSpec((tm, 512), lambda i: (i, Z_AK // 512)),
                  pl.BlockSpec((tm, 512), lambda i: (i, Z_AV // 512)),
                  pl.BlockSpec((tm, 512), tab), pl.BlockSpec((tm, 512), tab),
                  full(1, 512), full(1, 512), full(LANES, LANES)],
        out_specs=[pl.BlockSpec((tm, 512), lambda i: (i, 0)), pl.BlockSpec((tm, 512), lambda i: (i, 0)),
                   pl.BlockSpec((A_HEADS, 1, A_DV + V_PAD, tm), lambda i: (0, i, 0, 0))],
        out_shape=[jax.ShapeDtypeStruct((rows, 512), BF16), jax.ShapeDtypeStruct((rows, 512), BF16),
                   jax.ShapeDtypeStruct((A_HEADS, rows // tm, A_DV + V_PAD, tm), BF16)],
        compiler_params=pltpu.CompilerParams(dimension_semantics=("arbitrary",)),
        name='attn_prep',
    )(z, z, z, cos_t, sin_t, qn, kn, _group_ones(A_DH))


def _attn_kernel(*refs, n_lat, lam_init):
    lam_ref, sub_ref, q_ref, kc_ref, vc_ref = refs[:5]
    if n_lat:
        kl_ref, vl_ref = refs[5:7]
        o_ref, m_scr, acc_scr, *s_scr = refs[7:]
    else:
        o_ref, m_scr, acc_scr = refs[5:]

    q = q_ref[...]
    lane = lax.broadcasted_iota(jnp.int32, (1, LANES), 1)
    zero = jnp.zeros_like(q)
    qs = (jnp.where(lane < A_DH, q, zero), jnp.where(lane >= A_DH, q, zero))
    m_scr[...] = jnp.full_like(m_scr, -jnp.inf)
    acc_scr[...] = jnp.zeros_like(acc_scr)

    def scores(kb):
        return [_dot_nt(kb, qm) for qm in qs]

    def absorb(s, smax, vt):
        m_old = [m_scr[i] for i in range(2)]
        m_new = [jnp.maximum(m_old[i], smax[i]) for i in range(2)]
        p = [jnp.exp2(s[i] - m_new[i]).astype(BF16) for i in range(2)]
        pv = [_dot(vt, p[i]) for i in range(2)]
        for i in range(2):
            acc_scr[i] = jnp.exp2(m_old[i] - m_new[i]) * acc_scr[i] + pv[i]
            m_scr[i] = m_new[i]

    colmax = lambda s: [jnp.max(a, axis=0, keepdims=True) for a in s]
    s_ctx = scores(kc_ref[...])
    absorb(s_ctx, colmax(s_ctx), vc_ref[0, 0])
    if n_lat:
        def keys(c):
            return kl_ref[pl.ds(pl.multiple_of(c * KV_TILE, KV_TILE), KV_TILE), :]

        nbuf = len(s_scr) // 2
        s_buf, max_buf = s_scr[:nbuf], s_scr[nbuf:]

        def put(slot, s):
            for i in range(2):
                s_buf[slot][i] = s[i]
                max_buf[slot][i] = jnp.max(s[i], axis=0, keepdims=True)

        put(0, scores(keys(0)))

        unroll = min(ATT_UNROLL, n_lat)

        def body(j, carry):
            c = unroll * j
            for u in range(unroll):
                put((u + 1) % nbuf, scores(keys(jnp.minimum(c + u + 1, n_lat - 1))))
                slot = u % nbuf
                absorb([s_buf[slot][0], s_buf[slot][1]], [max_buf[slot][0], max_buf[slot][1]], vl_ref[0, c + u])
            return carry
        lax.fori_loop(0, n_lat // unroll, body, 0)

    lp = lam_ref[...]
    lam = (jnp.exp(jnp.sum(lp[0:1, :] * lp[1:2, :], axis=1, keepdims=True))
           - jnp.exp(jnp.sum(lp[2:3, :] * lp[3:4, :], axis=1, keepdims=True)) + lam_init)
    a0, a1 = acc_scr[0], acc_scr[1]
    o = a0[:A_DV] / a0[A_DV:A_DV + 1] - lam * (a1[:A_DV] / a1[A_DV:A_DV + 1])
    ms = jnp.mean(o * o, axis=0, keepdims=True)
    o = o * lax.rsqrt(ms + NORM_EPS) * sub_ref[...] * (1.0 - lam_init)
    o_ref[...] = o.T


def _attention(qr, kr, vt, lam_p, subln, lam_init, geom, batch, ctx_queries):
    nl, seq, ctx = geom
    ctx_blk = nl // ctx
    per_tile = KV_TILE // ctx
    full = lambda *s: pl.BlockSpec(s, lambda b, h, i: (0,) * len(s))
    kv_ctx = [pl.BlockSpec((ctx, LANES), lambda b, h, i: (ctx_blk + b, h)),
              pl.BlockSpec((1, 1, A_DV + V_PAD, ctx),
                           lambda b, h, i: (h, nl // KV_TILE + b // per_tile, 0, lax.rem(b, per_tile)))]
    head = [full(4, A_DH), full(A_DV, 1)]
    scratch = lambda tq: [pltpu.VMEM((2, 1, tq), F32), pltpu.VMEM((2, A_DV + V_PAD, tq), F32)]
    if ctx_queries:
        tq, nq, n_lat, out_rows = ctx, 1, 0, batch * ctx
        in_specs = head + [pl.BlockSpec((tq, LANES), lambda b, h, i: (ctx_blk + b, h))] + kv_ctx
        args = (lam_p, subln.reshape(A_DV, 1), qr, kr, vt)
        omap = lambda b, h, i: (b, h)
        scratch_shapes = scratch(tq)
    else:
        tq, out_rows = Q_TILE, nl
        nq, n_lat = seq // tq, seq // KV_TILE
        assert n_lat % min(ATT_UNROLL, n_lat) == 0 and min(ATT_UNROLL, n_lat) % ATT_BUFS == 0
        omap = lambda b, h, i: (b * nq + i, h)
        in_specs = head + [pl.BlockSpec((tq, LANES), omap)] + kv_ctx + [
            pl.BlockSpec((seq, LANES), lambda b, h, i: (b, h)),
            pl.BlockSpec((1, n_lat, A_DV + V_PAD, KV_TILE), lambda b, h, i: (h, b, 0, 0))]
        args = (lam_p, subln.reshape(A_DV, 1), qr, kr, vt, kr, vt)
        scratch_shapes = (scratch(tq) + [pltpu.VMEM((2, KV_TILE, tq), F32)] * ATT_BUFS
                          + [pltpu.VMEM((2, 1, tq), F32)] * ATT_BUFS)
    return pl.pallas_call(
        functools.partial(_attn_kernel, n_lat=n_lat, lam_init=lam_init),
        grid=(batch, A_HEADS, nq),
        in_specs=in_specs,
        out_specs=pl.BlockSpec((tq, LANES), omap),
        out_shape=jax.ShapeDtypeStruct((out_rows, MIX_W), F32),
        scratch_shapes=scratch_shapes,
        compiler_params=pltpu.CompilerParams(dimension_semantics=("arbitrary", "arbitrary", "arbitrary")),
        name='diff_attn_ctx' if ctx_queries else 'diff_attn',
    )(*args)


def _merge_kernel(x_ref, mod_ref, hm_ref, mo_ref, on_ref, yr_ref, ya_ref, g_ref, wb_ref, wo_ref, o_ref):
    hm = hm_ref[0] + hm_ref[1]
    on = on_ref[...]
    parts = []
    for h in range(M_HEADS):
        blk = hm[:, h * M_DV:(h + 1) * M_DV]
        ms = jnp.mean(blk * blk, axis=-1, keepdims=True)
        parts.append(blk * lax.rsqrt(ms + NORM_EPS) * on[:, h * M_DV:(h + 1) * M_DV])
    ym = jnp.concatenate(parts, axis=1) * _sigmoid(mo_ref[...].astype(F32))
    d = x_ref.shape[1]
    zsum = _sigmoid(g_ref[:, 0:d].astype(F32)) *_dot(ym.astype(BF16), wb_ref[0])
    zsum += _sigmoid(g_ref[:, d:2 * d].astype(F32)) *_dot(yr_ref[...].astype(BF16), wb_ref[1])
    zsum += _sigmoid(g_ref[:, 2 * d:3 * d].astype(F32)) *_dot(ya_ref[...].astype(BF16), wb_ref[2])
    o_ref[...] = x_ref[...] + mod_ref[0, 5:6, :] * _dot(zsum.astype(BF16), wo_ref[...])


def _merge(xr, mods, hm, z, out_norm, yr, ya, w_branch, w_o, rows, stream_of_tile):
    d = xr.shape[1]
    tm = ROW_TILE
    full = lambda *s: pl.BlockSpec(s, lambda i: (0,) * len(s))
    return pl.pallas_call(
        _merge_kernel,
        grid=(rows // tm,),
        in_specs=[pl.BlockSpec((tm, d), lambda i: (i, 0)),
                  pl.BlockSpec((1, N_ADA, d), lambda i: (stream_of_tile(i), 0, 0)),
                  pl.BlockSpec((2, tm, MIX_W), lambda i: (0, i, 0)),
                  pl.BlockSpec((tm, MIX_W), lambda i: (i, Z_MO // MIX_W)),
                  full(1, MIX_W),
                  pl.BlockSpec((tm, MIX_W), lambda i: (i, 0)),
                  pl.BlockSpec((tm, MIX_W), lambda i: (i, 0)),
                  pl.BlockSpec((tm, 3 * d), lambda i: (i, Z_G // (3 * d))),
                  full(3, MIX_W, d), full(d, d)],
        out_specs=pl.BlockSpec((tm, d), lambda i: (i, 0)),
        out_shape=jax.ShapeDtypeStruct((rows, d), F32),
        compiler_params=pltpu.CompilerParams(dimension_semantics=("arbitrary",)),
        name='merge',
    )(xr, mods, hm, z, out_norm.reshape(1, MIX_W), yr, ya, z, w_branch, w_o)


def _lambda_init(layer):
    return 0.8 - 0.6 * math.exp(-0.3 * layer)


def kernel(x, c, ctx, c_ctx, w_ada, b_ada, norm_g, ffn1_w_gu, ffn1_w_down, ffn2_w_gu, ffn2_w_down,
           w_in, m_conv, m_gate_bias, m_out_norm, r_conv, r_w0, r_w2, r_a0, r_a2, r_g2, r_kk, r_ka,
           r_rk, r_ln_w, r_ln_b, a_qk_norm, a_lambda, a_subln, w_branch, w_o):
    batch, seq, d = x.shape
    ctx_len = ctx.shape[1]
    depth = w_ada.shape[0]
    nl, nc = batch * seq, batch * ctx_len
    rows = nl + nc
    geom = (nl, seq, ctx_len)
    tm = ROW_TILE
    assert seq % tm == 0 and nc % tm == 0 and KV_TILE == tm and KV_TILE % ctx_len == 0 and nl % ctx_len == 0
    assert ctx_len % CHUNK == 0 and seq % GRID_W == 0 and d == 1024

    tiles_per_seq = seq // tm
    stream_of_tile = lambda i: jnp.minimum(i // tiles_per_seq, batch)

    cv = jnp.zeros((SUBLANES, d), F32).at[:batch].set(c).at[batch].set(c_ctx)
    mods_all = _ada_all(cv, w_ada, b_ada)[:, :batch + 1].reshape(depth, batch + 1, N_ADA, d)

    zcols = _z_column_sources()
    zsrc = jnp.asarray(np.maximum(zcols, 0))
    zmask = jnp.asarray((zcols >= 0).astype(np.float32))
    cos_t, sin_t = _rope_tables(seq, tm)

    xr = jnp.concatenate([x.reshape(nl, d), ctx.reshape(nc, d)], axis=0)
    for li in range(depth):
        last = li == depth - 1
        mods = mods_all[li]
        w_z = (jnp.take(w_in[li], zsrc, axis=1) * zmask).astype(BF16)
        w_z = w_z.reshape(d, Z_W // Z_TILE, Z_TILE).transpose(1, 0, 2)

        xr = _ffn(xr, mods, norm_g[li, 0], ffn1_w_gu[li].astype(BF16), ffn1_w_down[li].astype(BF16),
                  0, rows, stream_of_tile)
        z = _inproj(xr, mods, norm_g[li, 1], w_z, stream_of_tile)

        qk_act, gcol = _mprep(z, m_conv[li], m_gate_bias[li], geom)
        qk_hm = qk_act.reshape(rows, 2 * M_HEADS, M_DQK).transpose(1, 0, 2)
        grow = gcol[:, :16].reshape(rows // CHUNK, CHUNK, 16).transpose(0, 2, 1)
        hm = _mlstm_scan(qk_hm, z, gcol, grow, geom, batch)

        rkv_c, g_r, vb, *scan_in = _rprep(z, r_conv[li], r_w0[li], r_w2[li], r_a0[li], r_a2[li], r_g2[li],
                                           r_kk[li], r_ka[li], geom)
        y_tm = _rwkv_scan(*scan_in[:6], vb, scan_in[6], geom, batch)
        yr = _rpost(y_tm, rkv_c, g_r, r_ln_w[li], r_ln_b[li], r_rk[li])

        qr, kr, vt = _aprep(z, cos_t, sin_t, a_qk_norm[li], geom)
        att = lambda cq: _attention(qr, kr, vt, a_lambda[li], a_subln[li], _lambda_init(li), geom, batch, cq)
        ya = att(False) if last else jnp.concatenate([att(False), att(True)], axis=0)

        out_rows = nl if last else rows
        xr = _merge(xr, mods, hm, z, m_out_norm[li], yr, ya, w_branch[li].astype(BF16),
                    w_o[li].astype(BF16), out_rows, stream_of_tile)
        xr = _ffn(xr, mods, norm_g[li, 2], ffn2_w_gu[li].astype(BF16), ffn2_w_down[li].astype(BF16),
                  6, out_rows, stream_of_tile)
    return xr[:nl].reshape(batch, seq, d)
```

```python
import functools
import math

import numpy as np
import jax
import jax.numpy as jnp
from jax import lax
from jax.experimental import pallas as pl
from jax.experimental.pallas import tpu as pltpu

F32 = jnp.float32
BF16 = jnp.bfloat16
HI = lax.Precision.HIGHEST

N_ADA = 9
MACARON_W = 0.5
NORM_EPS = 1e-6
MIX_W = 512
GRID_W = 64
CHUNK = 64
SCAN_CHUNKS = 4
M_SCAN_CHUNKS = 4

M_HEADS = 4
M_DQK = 64
M_DV = 128
R_HEADS = 8
R_DH = 64
R_GN_EPS = 64e-5
A_HEADS = 4
A_DH = 64
A_DV = 128
ROPE_BASE = 10000.0

LANES = 128
SUBLANES = 8
HALO_ROWS = 16
ROW_TILE = 512
RWKV_PREP_TILE = 256
Q_TILE = 512
KV_TILE = 512
V_PAD = 16
LOG2_E = math.log2(math.e)
ATT_BUFS = 4
ATT_UNROLL = 16
FFN_CHUNK = 256
FFN_VMEM_LIMIT = 48 * 2 ** 20

_IN_SPLITS = (
    ('m_q', 256), ('m_k', 256), ('m_v', 512), ('m_o', 512),
    ('m_if', 4), ('m_ff', 4), ('m_ib', 4), ('m_fb', 4),
    ('r_r', 512), ('r_k', 512), ('r_v', 512),
    ('r_wf', 64), ('r_wb', 64), ('r_af', 64), ('r_ab', 64), ('r_g', 128),
    ('a_q', 512), ('a_k', 512), ('a_v', 512),
    ('g_m', 1024), ('g_r', 1024), ('g_a', 1024),
)
Z_G, Z_MQK, Z_MV, Z_MO, Z_RKV, Z_AQ, Z_AK, Z_AV, Z_LORA, Z_MG, Z_W = (
    0, 3072, 3584, 4096, 4608, 6144, 6656, 7168, 7680, 8064, 8192)
Z_TILE = 2048


def _deinterleave64():
    return np.concatenate([np.arange(0, 64, 2), np.arange(1, 64, 2)])


def _z_column_sources():
    off, start = {}, 0
    for name, w in _IN_SPLITS:
        off[name] = start
        start += w
    cols = []
    rng = lambda n: list(range(off[n], off[n] + dict(_IN_SPLITS)[n]))
    cols += rng('g_m') + rng('g_r') + rng('g_a')
    cols += rng('m_q') + rng('m_k') + rng('m_v') + rng('m_o')
    cols += rng('r_r') + rng('r_k') + rng('r_v')
    perm = _deinterleave64()
    for n in ('a_q', 'a_k'):
        for g in range(8):
            cols += list(off[n] + g * 64 + perm)
    cols += rng('a_v')
    cols += rng('r_wf') + rng('r_wb') + rng('r_af') + rng('r_ab') + rng('r_g')
    cols += rng('m_if') + rng('m_ff') + rng('m_ib') + rng('m_fb')
    cols += [-1] * (Z_W - len(cols))
    assert len(cols) == Z_W
    return np.asarray(cols, np.int32)


def _sigmoid(x):
    return 1.0 / (1.0 + jnp.exp(-x))


def _dot(a, b, prec=None):
    return jnp.dot(a, b, preferred_element_type=F32, precision=prec)


def _dot_nt(a, b, prec=None):
    return lax.dot_general(a, b, (((1,), (1,)), ((), ())), preferred_element_type=F32, precision=prec)


def _dot_tn(a, b, prec=None):
    return lax.dot_general(a, b, (((0,), (0,)), ((), ())), preferred_element_type=F32, precision=prec)


def _norm_mod(x, g, shift, scale):
    ms = jnp.mean(x * x, axis=-1, keepdims=True)
    y = x * lax.rsqrt(ms + NORM_EPS) * g
    return y * (1.0 + scale) + shift


def _ada_kernel(c_ref, w_ref, b_ref, o_ref):
    c = c_ref[...]
    s = c * _sigmoid(c)
    o_ref[0] = _dot(s, w_ref[0], HI) + b_ref[0]


def _ada_all(cv, w_ada, b_ada):
    depth, d, nd = w_ada.shape
    tn = 1024
    return pl.pallas_call(
        _ada_kernel,
        grid=(depth, nd // tn),
        in_specs=[pl.BlockSpec((SUBLANES, d), lambda l, n: (0, 0)),
                  pl.BlockSpec((1, d, tn), lambda l, n: (l, 0, n)),
                  pl.BlockSpec((1, 1, tn), lambda l, n: (l, 0, n))],
        out_specs=pl.BlockSpec((1, SUBLANES, tn), lambda l, n: (l, 0, n)),
        out_shape=jax.ShapeDtypeStruct((depth, SUBLANES, nd), F32),
        name='ada',
    )(cv, w_ada, b_ada.reshape(depth, 1, nd))


def _ffn_kernel(x_ref, mod_ref, g_ref, wgu_ref, wd_ref, o_ref, u_scr, *, mi):
    dff = wd_ref.shape[0]
    h = _norm_mod(x_ref[...], g_ref[...], mod_ref[0, mi:mi + 1, :], mod_ref[0, mi + 1:mi + 2, :]).astype(BF16)
    for j in range(dff // FFN_CHUNK):
        lo, hi = j * FFN_CHUNK, (j + 1) * FFN_CHUNK
        a = _dot(h, wgu_ref[:, lo:hi])
        b = _dot(h, wgu_ref[:, dff + lo:dff + hi])
        u_scr[:, lo:hi] = ((a * _sigmoid(a)) * b).astype(BF16)
    o_ref[...] = x_ref[...] + MACARON_W * mod_ref[0, mi + 2:mi + 3, :] * _dot(u_scr[...], wd_ref[...])


def _ffn(xr, mods, g, w_gu, w_down, mi, rows, stream_of_tile):
    d = xr.shape[1]
    dff = w_down.shape[0]
    assert dff % FFN_CHUNK == 0
    tm = ROW_TILE
    resident = lambda *s: pl.BlockSpec(s, lambda i: (0,) * len(s), pipeline_mode=pl.Buffered(1))
    return pl.pallas_call(
        functools.partial(_ffn_kernel, mi=mi),
        grid=(rows // tm,),
        in_specs=[pl.BlockSpec((tm, d), lambda i: (i, 0)),
                  pl.BlockSpec((1, N_ADA, d), lambda i: (stream_of_tile(i), 0, 0)),
                  pl.BlockSpec((1, d), lambda i: (0, 0)),
                  resident(d, 2 * dff), resident(dff, d)],
        out_specs=pl.BlockSpec((tm, d), lambda i: (i, 0)),
        out_shape=jax.ShapeDtypeStruct((rows, d), F32),
        scratch_shapes=[pltpu.VMEM((tm, dff), BF16)],
        compiler_params=pltpu.CompilerParams(dimension_semantics=("arbitrary",),
                                             vmem_limit_bytes=FFN_VMEM_LIMIT),
        name='ffn',
    )(xr, mods, g.reshape(1, d), w_gu, w_down)


def _inproj_kernel(x_ref, mod_ref, g_ref, w_ref, o_ref, h_scr):
    @pl.when(pl.program_id(1) == 0)
    def _():
        h = _norm_mod(x_ref[...], g_ref[...], mod_ref[0, 3:4, :], mod_ref[0, 4:5, :])
        h_scr[...] = h.astype(BF16)

    o_ref[...] = _dot(h_scr[...], w_ref[pl.program_id(1)]).astype(o_ref.dtype)


def _inproj(xr, mods, g, w_z, stream_of_tile):
    rows, d = xr.shape
    tm = ROW_TILE
    nz = Z_W // Z_TILE
    return pl.pallas_call(
        _inproj_kernel,
        grid=(rows // tm, nz),
        in_specs=[pl.BlockSpec((tm, d), lambda i, n: (i, 0)),
                  pl.BlockSpec((1, N_ADA, d), lambda i, n: (stream_of_tile(i), 0, 0)),
                  pl.BlockSpec((1, d), lambda i, n: (0, 0)),
                  pl.BlockSpec((nz, d, Z_TILE), lambda i, n: (0, 0, 0), pipeline_mode=pl.Buffered(1))],
        out_specs=pl.BlockSpec((tm, Z_TILE), lambda i, n: (i, n)),
        out_shape=jax.ShapeDtypeStruct((rows, Z_W), BF16),
        scratch_shapes=[pltpu.VMEM((tm, d), BF16)],
        compiler_params=pltpu.CompilerParams(dimension_semantics=("arbitrary", "arbitrary"),
                                             vmem_limit_bytes=FFN_VMEM_LIMIT),
        name='inproj',
    )(xr, mods, g.reshape(1, d), w_z)


def _conv3(x, prev_rows, next_rows, w, row0, geom):
    nl, seq, ctx = geom
    tm = x.shape[0]
    x = x.astype(F32)
    t = lax.broadcasted_iota(jnp.int32, (tm, 1), 0)
    r = row0 + t
    is_lat = r < nl
    pos = jnp.where(is_lat, lax.rem(r, seq), lax.rem(jnp.maximum(r - nl, 0), ctx))
    seglen = jnp.where(is_lat, seq, ctx)
    xm = pltpu.roll(x, 1, 0)
    xm = jnp.where(t == 0, prev_rows[HALO_ROWS - 1:HALO_ROWS, :].astype(F32), xm)
    xm = jnp.where(pos == 0, 0.0, xm)
    xp = pltpu.roll(x, tm - 1, 0)
    xp = jnp.where(t == tm - 1, next_rows[0:1, :].astype(F32), xp)
    xp = jnp.where(pos == seglen - 1, 0.0, xp)
    return xm * w[0:1, :] + x * w[1:2, :] + xp * w[2:3, :]


def _halo_specs(tm, width, col_blk, rows):
    per = tm // HALO_ROWS
    last = rows // HALO_ROWS - 1
    return [pl.BlockSpec((tm, width), lambda i: (i, col_blk)),
            pl.BlockSpec((HALO_ROWS, width), lambda i: (jnp.maximum(i * per - 1, 0), col_blk)),
            pl.BlockSpec((HALO_ROWS, width), lambda i: (jnp.minimum((i + 1) * per, last), col_blk))]


def _mprep_kernel(x_ref, xp_ref, xn_ref, w_ref, gt_ref, gb_ref, qk_ref, go_ref, *, geom):
    tm = x_ref.shape[0]
    y = _conv3(x_ref[...], xp_ref[...], xn_ref[...], w_ref[...], pl.program_id(0) * tm, geom)
    y = y * _sigmoid(y)
    lane = lax.broadcasted_iota(jnp.int32, (1, y.shape[1]), 1)
    qk_ref[...] = jnp.where(lane >= M_HEADS * M_DQK, y * (M_DQK ** -0.5), y)
    g = gt_ref[...].astype(F32) + gb_ref[...]
    gl = lax.broadcasted_iota(jnp.int32, (1, LANES), 1)
    is_forget = (lax.rem(gl, 2 * M_HEADS) >= M_HEADS) & (gl < 4 * M_HEADS)
    logsig = jnp.minimum(g, 0.0) - jnp.log(1.0 + jnp.exp(-jnp.abs(g)))
    go_ref[...] = jnp.where(is_forget, logsig, g)


def _mprep(z, conv_w, gate_bias, geom):
    rows = z.shape[0]
    tm = ROW_TILE
    gb = jnp.zeros((1, LANES), F32).at[0, :4 * M_HEADS].set(gate_bias.reshape(-1))
    return pl.pallas_call(
        functools.partial(_mprep_kernel, geom=geom),
        grid=(rows // tm,),
        in_specs=_halo_specs(tm, 512, Z_MQK // 512, rows) + [
            pl.BlockSpec((3, 512), lambda i: (0, 0)),
            pl.BlockSpec((tm, LANES), lambda i: (i, Z_MG // LANES)),
            pl.BlockSpec((1, LANES), lambda i: (0, 0))],
        out_specs=[pl.BlockSpec((tm, 512), lambda i: (i, 0)),
                   pl.BlockSpec((tm, LANES), lambda i: (i, 0))],
        out_shape=[jax.ShapeDtypeStruct((rows, 512), F32), jax.ShapeDtypeStruct((rows, LANES), F32)],
        compiler_params=pltpu.CompilerParams(dimension_semantics=("arbitrary",)),
        name='mlstm_prep',
    )(z, z, z, conv_w, z, gb)


def _chunk_block(d, b, j, geom, size=CHUNK):
    nl, seq, ctx = geom
    nc_c, nc_l = ctx // size, seq // size
    jl = j - nc_c
    c_ctx = jnp.where(d == 0, j, nc_c - 1 - j)
    c_lat = jnp.where(d == 0, jl, nc_l - 1 - jl)
    return jnp.where(j < nc_c, nl // size + b * nc_c + c_ctx, b * nc_l + c_lat)


def _order_masks(d):
    ti = lax.broadcasted_iota(jnp.int32, (CHUNK, CHUNK), 0)
    si = lax.broadcasted_iota(jnp.int32, (CHUNK, CHUNK), 1)
    fwd = d == 0
    ahead = jnp.where(fwd, si - ti, ti - si)
    incl = ahead <= 0
    strict = ahead < 0
    incl_t = ahead >= 0
    return fwd, incl, strict, incl_t, ti == si


def _mlstm_kernel(qk_ref, v_ref, gc_ref, gr_ref, o_ref, ct_scr, m_scr, init=True):
    d = pl.program_id(0)

    if init:
        @pl.when(pl.program_id(2) == 0)
        def _():
            ct_scr[...] = jnp.zeros_like(ct_scr)
            m_scr[...] = jnp.zeros_like(m_scr)

    fwd, incl, _, incl_t, _ = _order_masks(d)
    heads = range(M_HEADS)
    chunks = range(M_SCAN_CHUNKS)
    lane = lax.broadcasted_iota(jnp.int32, (CHUNK, LANES), 1)
    ones_col = jnp.where(lane == 0, 1.0, 0.0)
    cat1 = lambda xs: jnp.concatenate(xs, axis=1)

    pre = []
    for u in chunks:
        cu = jnp.where(fwd, u, M_SCAN_CHUNKS - 1 - u)
        rows = pl.ds(pl.multiple_of(cu * CHUNK, CHUNK), CHUNK)
        gc = gc_ref[rows, :]
        gr = gr_ref[cu]
        q = [qk_ref[h, rows, :].astype(BF16) for h in heads]
        k = [qk_ref[M_HEADS + h, rows, :].astype(BF16) for h in heads]
        v = [v_ref[rows, h * M_DV:(h + 1) * M_DV].astype(F32) for h in heads]
        qk = [_dot_nt(q[h], k[h]) for h in heads]
        per_head = []
        for h in heads:
            i_col = jnp.where(fwd, gc[:, h:h + 1], gc[:, 8 + h:9 + h])
            f_col = jnp.where(fwd, gc[:, 4 + h:5 + h], gc[:, 12 + h:13 + h])
            i_row = jnp.where(fwd, gr[h:h + 1, :], gr[8 + h:9 + h, :])
            f_row = jnp.where(fwd, gr[4 + h:5 + h, :], gr[12 + h:13 + h, :])
            bcum_col = jnp.sum(jnp.where(incl, f_row, 0.0), axis=1, keepdims=True)
            bcum_row = jnp.sum(jnp.where(incl_t, f_col, 0.0), axis=0, keepdims=True)
            dlog = jnp.where(incl, bcum_col - bcum_row + i_row, -jnp.inf)
            rmax = jnp.max(dlog, axis=1, keepdims=True)
            btot = jnp.sum(f_col, axis=0, keepdims=True)
            wlog = btot - bcum_col + i_col
            per_head.append(dict(bcum=bcum_col, rmax=rmax, btot=btot, wlog=wlog,
                                 wlog_row=btot - bcum_row + i_row,
                                 wmax=jnp.max(wlog, axis=0, keepdims=True),
                                 qkd=(qk[h] * jnp.exp(dlog - rmax)).astype(BF16)))
        v_aug = [cat1([v[h], ones_col]) for h in heads]
        v_aug_b = [a.astype(BF16) for a in v_aug]
        sv0 = [_dot(per_head[h]['qkd'], v_aug_b[h]) for h in heads]
        cross = [[_dot_nt(q[h], p['k'][h]) for h in heads] for p in pre]
        pre.append(dict(rows=rows, q=q, k=k, v_aug=v_aug, v_aug_b=v_aug_b, sv0=sv0, cross=cross, g=per_head))

    c0 = [ct_scr[h] for h in heads]
    c0b = [a.astype(BF16) for a in c0]
    m_st = [m_scr[h] for h in heads]
    ct = list(c0)
    decay = [1.0 for _ in heads]
    earlier = []
    for u in chunks:
        c = pre[u]
        qc = [decay[h] * _dot(c['q'][h], c0b[h]) for h in heads]
        for e, w_rows in enumerate(earlier):
            for h in heads:
                qc[h] = qc[h] + _dot((c['cross'][e][h] * w_rows[h]).astype(BF16), pre[e]['v_aug_b'][h])
        m_new = [jnp.maximum(c['g'][h]['btot'] + m_st[h], c['g'][h]['wmax']) for h in heads]
        dec = [jnp.exp(c['g'][h]['btot'] + m_st[h] - m_new[h]) for h in heads]
        ws = [jnp.exp(c['g'][h]['wlog'] - m_new[h]) for h in heads]
        kv = [_dot_tn(c['k'][h], (ws[h] * c['v_aug'][h]).astype(BF16)) for h in heads]
        for h in heads:
            g = c['g'][h]
            inter = g['bcum'] + m_st[h]
            mt = jnp.maximum(inter, g['rmax'])
            nd = jnp.exp(g['rmax'] - mt) * c['sv0'][h] + jnp.exp(inter - mt) * qc[h]
            den = jnp.maximum(jnp.abs(nd[:, M_DV:M_DV + 1]), jnp.exp(-mt))
            o_ref[0, c['rows'], h * M_DV:(h + 1) * M_DV] = nd[:, :M_DV] / den
            ct[h] = dec[h] * ct[h] + kv[h]
        earlier = [[w_rows[h] * dec[h] for h in heads] for w_rows in earlier]
        earlier.append([jnp.exp(c['g'][h]['wlog_row'] - m_new[h]) for h in heads])
        decay = [decay[h] * dec[h] for h in heads]
        m_st = m_new
    for h in heads:
        ct_scr[h] = ct[h]
        m_scr[h] = m_st[h]


def _mlstm_scan(qk_hm, z, gcol, grow, geom, batch):
    rows = z.shape[0]
    step = M_SCAN_CHUNKS * CHUNK
    assert geom[1] % step == 0 and geom[2] % step == 0
    nch = (geom[1] + geom[2]) // step
    blk = lambda d, b, j: _chunk_block(d, b, j, geom, step)
    return pl.pallas_call(
        _mlstm_kernel,
        grid=(2, batch, nch),
        in_specs=[pl.BlockSpec((2 * M_HEADS, step, M_DQK), lambda d, b, j: (0, blk(d, b, j), 0)),
                  pl.BlockSpec((step, MIX_W), lambda d, b, j: (blk(d, b, j), Z_MV // MIX_W)),
                  pl.BlockSpec((step, LANES), lambda d, b, j: (blk(d, b, j), 0)),
                  pl.BlockSpec((M_SCAN_CHUNKS, 16, CHUNK), lambda d, b, j: (blk(d, b, j), 0, 0))],
        out_specs=pl.BlockSpec((1, step, MIX_W), lambda d, b, j: (d, blk(d, b, j), 0)),
        out_shape=jax.ShapeDtypeStruct((2, rows, MIX_W), F32),
        scratch_shapes=[pltpu.VMEM((M_HEADS, M_DQK, M_DV + LANES), F32),
                        pltpu.VMEM((M_HEADS, 1, 1), F32)],
        compiler_params=pltpu.CompilerParams(dimension_semantics=("arbitrary", "arbitrary", "arbitrary")),
        name='mlstm_scan',
    )(qk_hm, z, gcol, grow)


def _rprep_kernel(x_ref, xp_ref, xn_ref, cw_ref, lo_ref, w0_ref, w2_ref, a0_ref, a2_ref, g2_ref,
                  kkw_ref, kaw_ref, bd_ref, tri_ref,
                  rkv_ref, g_ref, vb_ref, at_ref, bt_ref, kt_ref, rt_ref, bb_ref, kb_ref, et_ref, *, geom):
    tm = x_ref.shape[0]
    rkv = _conv3(x_ref[...], xp_ref[...], xn_ref[...], cw_ref[...], pl.program_id(0) * tm, geom)
    rkv_ref[...] = rkv
    r = rkv[:, 0:MIX_W]
    k = rkv[:, MIX_W:2 * MIX_W]
    vb_ref[...] = rkv[:, 2 * MIX_W:3 * MIX_W].astype(BF16)
    lo = lo_ref[...].astype(F32)
    g_ref[...] = _dot(_sigmoid(lo[:, 256:384]), g2_ref[...], HI)
    kkr = k * kkw_ref[...]
    kk = kkr / jnp.maximum(jnp.sqrt(_group_sum(kkr * kkr, bd_ref[...])), 1e-12)
    for d in range(2):
        w_raw = w0_ref[d:d + 1, :] + _dot(jnp.tanh(lo[:, 64 * d:64 * d + 64]), w2_ref[d], HI)
        lw = -_sigmoid(w_raw) * math.exp(-0.5)
        a = _sigmoid(a0_ref[d:d + 1, :] + _dot(lo[:, 128 + 64 * d:192 + 64 * d], a2_ref[d], HI))
        hi = lw.astype(BF16)
        rem = lw - hi.astype(F32)
        mid = rem.astype(BF16)
        low = (rem - mid.astype(F32)).astype(BF16)
        parts = jnp.concatenate([hi, mid, low], axis=1)
        c3 = _dot(tri_ref[d, 0], parts)
        s3 = _dot(tri_ref[d, 1], parts)
        cum = c3[:, 0:MIX_W] + c3[:, MIX_W:2 * MIX_W] + c3[:, 2 * MIX_W:3 * MIX_W]
        suf = s3[:, 0:MIX_W] + s3[:, MIX_W:2 * MIX_W] + s3[:, 2 * MIX_W:3 * MIX_W]
        kd = k * (1.0 + (a - 1.0) * kaw_ref[...])
        kka = kk * a
        e_neg = jnp.exp(-cum)
        e_end = jnp.exp(suf)
        at_ref[d] = (-kk * jnp.exp(cum - lw)).astype(BF16)
        bt_ref[d] = (kka * e_neg).astype(BF16)
        kt_ref[d] = (kd * e_neg).astype(BF16)
        rt_ref[d] = (r * jnp.exp(cum)).astype(BF16)
        bb_ref[d] = (kka * e_end).astype(BF16)
        kb_ref[d] = (kd * e_end).astype(BF16)
        et_ref[d] = jnp.exp(cum + suf)


def _chunk_order_matrices(tm):
    i = np.arange(tm)
    same = (i[:, None] // CHUNK) == (i[None, :] // CHUNK)
    le = i[None, :] <= i[:, None]
    ge = i[None, :] >= i[:, None]
    mats = np.stack([np.stack([same & le, same & ~le]), np.stack([same & ge, same & ~ge])])
    return jnp.asarray(mats.astype(np.float32), dtype=BF16)


def _rprep(z, conv_w, w0, w2, a0, a2, g2, kk_w, ka_w, geom):
    rows = z.shape[0]
    tm = RWKV_PREP_TILE
    full = lambda *s: pl.BlockSpec(s, lambda i: (0,) * len(s))
    row = lambda w: pl.BlockSpec((tm, w), lambda i: (i, 0))
    both = pl.BlockSpec((2, tm, MIX_W), lambda i: (0, i, 0))
    shp = lambda dt: jax.ShapeDtypeStruct((2, rows, MIX_W), dt)
    return pl.pallas_call(
        functools.partial(_rprep_kernel, geom=geom),
        grid=(rows // tm,),
        in_specs=_halo_specs(tm, 1536, Z_RKV // 1536, rows) + [
            full(3, 1536),
            pl.BlockSpec((tm, 384), lambda i: (i, Z_LORA // 384)),
            full(2, 512), full(2, 64, 512), full(2, 512), full(2, 64, 512), full(128, 512),
            full(1, MIX_W), full(1, MIX_W), full(LANES, LANES), full(2, 2, tm, tm)],
        out_specs=[row(1536), row(MIX_W), row(MIX_W)] + [both] * 7,
        out_shape=[jax.ShapeDtypeStruct((rows, 1536), F32), jax.ShapeDtypeStruct((rows, MIX_W), F32),
                   jax.ShapeDtypeStruct((rows, MIX_W), BF16)] + [shp(BF16)] * 6 + [shp(F32)],
        compiler_params=pltpu.CompilerParams(dimension_semantics=("arbitrary",)),
        name='rwkv_prep',
    )(z, z, z, conv_w, z, w0, w2, a0, a2, g2, kk_w.reshape(1, MIX_W), ka_w.reshape(1, MIX_W),
      _group_ones(R_DH), _chunk_order_matrices(tm))


def _rwkv_kernel(at_ref, bt_ref, kt_ref, rt_ref, bb_ref, kb_ref, v_ref, et_ref, y_ref, ss_scr, init=True):
    d = pl.program_id(0)

    if init:
        @pl.when(pl.program_id(2) == 0)
        def _():
            ss_scr[...] = jnp.zeros_like(ss_scr)

    ti = lax.broadcasted_iota(jnp.int32, (CHUNK, LANES), 0)
    li = lax.broadcasted_iota(jnp.int32, (CHUNK, LANES), 1)
    si = jnp.bitwise_and(li, R_DH - 1)
    ahead = jnp.where(d == 0, si - ti, ti - si)
    strict2 = ahead < 0
    incl2 = ahead <= 0
    first = li < R_DH
    eye64 = (lax.broadcasted_iota(jnp.int32, (CHUNK, R_DH), 0)
             == lax.broadcasted_iota(jnp.int32, (CHUNK, R_DH), 1)).astype(F32)
    rr = lax.broadcasted_iota(jnp.int32, (LANES, LANES), 0)
    cc = lax.broadcasted_iota(jnp.int32, (LANES, LANES), 1)
    same_head = (rr < R_DH) == (cc < R_DH)
    eye128 = rr == cc
    zero = jnp.zeros((CHUNK, LANES), BF16)
    cat0 = lambda xs: jnp.concatenate(xs, axis=0)
    cat1 = lambda xs: jnp.concatenate(xs, axis=1)
    keep = lambda h, x: jnp.where(first, x, zero) if h == 0 else jnp.where(first, zero, x)

    pairs = range(R_HEADS // 2)
    items = [(u, p) for u in range(SCAN_CHUNKS) for p in pairs]
    rows_of = [pl.ds(pl.multiple_of(jnp.where(d == 0, u, SCAN_CHUNKS - 1 - u) * CHUNK, CHUNK), CHUNK)
               for u in range(SCAN_CHUNKS)]
    lanes_of = [slice(p * LANES, (p + 1) * LANES) for p in pairs]
    at = [at_ref[0, rows_of[u], lanes_of[p]] for u, p in items]
    rt = [rt_ref[0, rows_of[u], lanes_of[p]] for u, p in items]
    v = [v_ref[rows_of[u], lanes_of[p]] for u, p in items]
    n = range(len(items))
    g = [_dot_nt(cat0([keep(0, at[i]), keep(0, rt[i]), keep(1, at[i]), keep(1, rt[i])]),
                 cat0([bt_ref[0, rows_of[u], lanes_of[p]], kt_ref[0, rows_of[u], lanes_of[p]]]))
         for i, (u, p) in enumerate(items)]
    ga = [[jnp.where(strict2, g[i][2 * h * CHUNK:(2 * h + 1) * CHUNK], 0.0) for h in range(2)] for i in n]
    gr = [[jnp.where(incl2, g[i][(2 * h + 1) * CHUNK:(2 * h + 2) * CHUNK], 0.0) for h in range(2)] for i in n]
    aakv = [_dot(cat1(ga[i]).astype(BF16), cat0([zero, keep(0, v[i]), zero, keep(1, v[i])])).astype(BF16)
            for i in n]
    pw = [ga[i][h][:, :R_DH] for i in n for h in range(2)]
    tinv = [eye64 + a for a in pw]
    for _ in range(int(math.log2(CHUNK)) - 1):
        pwb = [a.astype(BF16) for a in pw]
        pw = [_dot(a, a) for a in pwb]
        tinv = [t + _dot(t.astype(BF16), a.astype(BF16)) for t, a in zip(tinv, pw)]
    tinv = [t.astype(BF16) for t in tinv]
    x = [(_dot(tinv[2 * i], cat1([keep(0, at[i]), keep(0, aakv[i])]))
          + _dot(tinv[2 * i + 1], cat1([keep(1, at[i]), keep(1, aakv[i])]))).astype(BF16) for i in n]
    y4 = [_dot(cat1(gr[i]).astype(BF16),
               cat0([cat1([keep(0, x[i][:, :LANES]), keep(0, x[i][:, LANES:])]), cat1([zero, keep(0, v[i])]),
                     cat1([keep(1, x[i][:, :LANES]), keep(1, x[i][:, LANES:])]), cat1([zero, keep(1, v[i])])]))
          for i in n]
    mn = [_dot_tn(cat0([bb_ref[0, rows_of[u], lanes_of[p]], kb_ref[0, rows_of[u], lanes_of[p]]]),
                  cat0([x[i], cat1([zero, v[i]])])) for i, (u, p) in enumerate(items)]
    lhs = []
    for i, (u, p) in enumerate(items):
        rh = rt[i].astype(F32) + y4[i][:, :LANES]
        decay = et_ref[0, pl.ds(jnp.where(d == 0, u, SCAN_CHUNKS - 1 - u) * CHUNK, 1), lanes_of[p]]
        mt = jnp.where(eye128, decay, 0.0) + jnp.where(same_head, mn[i][:, :LANES], 0.0)
        lhs.append(cat0([rh, mt]).astype(BF16))
    ss = [ss_scr[p] for p in pairs]
    for u in range(SCAN_CHUNKS):
        out = [_dot(lhs[u * len(pairs) + p], ss[p].astype(BF16)) for p in pairs]
        for p in pairs:
            i = u * len(pairs) + p
            y_ref[0, rows_of[u], lanes_of[p]] = out[p][:CHUNK] + y4[i][:, LANES:]
            ss[p] = out[p][CHUNK:] + jnp.where(same_head, mn[i][:, LANES:], 0.0)
    for p in pairs:
        ss_scr[p] = ss[p]


def _rwkv_scan(at, bt, kt, rt, bb, kb, vb, et, geom, batch):
    rows = vb.shape[0]
    step = SCAN_CHUNKS * CHUNK
    assert geom[1] % step == 0 and geom[2] % step == 0
    nch = (geom[1] + geom[2]) // step
    blk = lambda d, b, j: _chunk_block(d, b, j, geom, step)
    per_dir = pl.BlockSpec((1, step, MIX_W), lambda d, b, j: (d, blk(d, b, j), 0))
    return pl.pallas_call(
        _rwkv_kernel,
        grid=(2, batch, nch),
        in_specs=[per_dir] * 6 + [pl.BlockSpec((step, MIX_W), lambda d, b, j: (blk(d, b, j), 0)), per_dir],
        out_specs=per_dir,
        out_shape=jax.ShapeDtypeStruct((2, rows, MIX_W), F32),
        scratch_shapes=[pltpu.VMEM((R_HEADS // 2, LANES, LANES), F32)],
        compiler_params=pltpu.CompilerParams(dimension_semantics=("arbitrary", "arbitrary", "arbitrary")),
        name='rwkv_scan',
    )(at, bt, kt, rt, bb, kb, vb, et)


def _both_scans_kernel(at_ref, bt_ref, kt_ref, rt_ref, bb_ref, kb_ref, v_ref, et_ref, qk_ref, mv_ref, gc_ref, gr_ref,
                       y_ref, o_ref, ss_scr, ct_scr, m_scr):
    @pl.when(pl.program_id(2) == 0)
    def _():
        ss_scr[...] = jnp.zeros_like(ss_scr)
        ct_scr[...] = jnp.zeros_like(ct_scr)
        m_scr[...] = jnp.zeros_like(m_scr)

    _rwkv_kernel(at_ref, bt_ref, kt_ref, rt_ref, bb_ref, kb_ref, v_ref, et_ref, y_ref, ss_scr, init=False)
    _mlstm_kernel(qk_ref, mv_ref, gc_ref, gr_ref, o_ref, ct_scr, m_scr, init=False)


def _both_scans(at, bt, kt, rt, bb, kb, vb, et, qk_hm, z, gcol, grow, geom, batch):
    rows = vb.shape[0]
    assert SCAN_CHUNKS == M_SCAN_CHUNKS
    step = SCAN_CHUNKS * CHUNK
    assert geom[1] % step == 0 and geom[2] % step == 0
    nch = (geom[1] + geom[2]) // step
    blk = lambda d, b, j: _chunk_block(d, b, j, geom, step)
    per_dir = pl.BlockSpec((1, step, MIX_W), lambda d, b, j: (d, blk(d, b, j), 0))
    return pl.pallas_call(
        _both_scans_kernel,
        grid=(2, batch, nch),
        in_specs=[per_dir] * 6 + [pl.BlockSpec((step, MIX_W), lambda d, b, j: (blk(d, b, j), 0)), per_dir] + [
            pl.BlockSpec((2 * M_HEADS, step, M_DQK), lambda d, b, j: (0, blk(d, b, j), 0)),
            pl.BlockSpec((step, MIX_W), lambda d, b, j: (blk(d, b, j), Z_MV // MIX_W)),
            pl.BlockSpec((step, LANES), lambda d, b, j: (blk(d, b, j), 0)),
            pl.BlockSpec((M_SCAN_CHUNKS, 16, CHUNK), lambda d, b, j: (blk(d, b, j), 0, 0))],
        out_specs=[per_dir, per_dir],
        out_shape=[jax.ShapeDtypeStruct((2, rows, MIX_W), F32)] * 2,
        scratch_shapes=[pltpu.VMEM((R_HEADS // 2, LANES, LANES), F32),
                        pltpu.VMEM((M_HEADS, M_DQK, M_DV + LANES), F32),
                        pltpu.VMEM((M_HEADS, 1, 1), F32)],
        compiler_params=pltpu.CompilerParams(dimension_semantics=("arbitrary", "arbitrary", "arbitrary")),
        name='both_scans',
    )(at, bt, kt, rt, bb, kb, vb, et, qk_hm, z, gcol, grow)


def _rpost_kernel(y_ref, rkv_ref, g_ref, lnw_ref, lnb_ref, rk_ref, bd_ref, o_ref):
    y = y_ref[0] + y_ref[1]
    bd = bd_ref[...]
    inv = 1.0 / R_DH
    mu = _group_sum(y, bd) * inv
    yc = y - mu
    var = _group_sum(yc * yc, bd) * inv
    yn = yc * lax.rsqrt(var + R_GN_EPS) * lnw_ref[...] + lnb_ref[...]
    r = rkv_ref[:, 0:512]
    k = rkv_ref[:, 512:1024]
    v = rkv_ref[:, 1024:1536]
    bonus = _group_sum(r * k * rk_ref[...], bd) * v
    o_ref[...] = (yn + bonus) * g_ref[...]


def _group_ones(group):
    i = np.arange(LANES) // group
    return jnp.asarray((i[:, None] == i[None, :]).astype(np.float32), dtype=BF16)


def _group_sum(x, ones):
    tm, w = x.shape
    nt = w // LANES
    xs = jnp.concatenate([x[:, t * LANES:(t + 1) * LANES] for t in range(nt)], axis=0)
    hi = xs.astype(BF16)
    rem = xs - hi.astype(F32)
    mid = rem.astype(BF16)
    low = (rem - mid.astype(F32)).astype(BF16)
    y = _dot(jnp.concatenate([hi, mid, low], axis=0), ones)
    n = nt * tm
    ys = y[0:n] + y[n:2 * n] + y[2 * n:3 * n]
    return jnp.concatenate([ys[t * tm:(t + 1) * tm] for t in range(nt)], axis=1)


def _rpost(y_tm, rkv_c, g, ln_w, ln_b, r_k):
    rows = rkv_c.shape[0]
    tm = ROW_TILE
    full = lambda *s: pl.BlockSpec(s, lambda i: (0,) * len(s))
    return pl.pallas_call(
        _rpost_kernel,
        grid=(rows // tm,),
        in_specs=[pl.BlockSpec((2, tm, 512), lambda i: (0, i, 0)),
                  pl.BlockSpec((tm, 1536), lambda i: (i, 0)),
                  pl.BlockSpec((tm, 512), lambda i: (i, 0)),
                  full(1, 512), full(1, 512), full(1, 512), full(LANES, LANES)],
        out_specs=pl.BlockSpec((tm, 512), lambda i: (i, 0)),
        out_shape=jax.ShapeDtypeStruct((rows, 512), F32),
        compiler_params=pltpu.CompilerParams(dimension_semantics=("arbitrary",)),
        name='rwkv_post',
    )(y_tm, rkv_c, g, ln_w.reshape(1, 512), ln_b.reshape(1, 512), r_k.reshape(1, 512),
      _group_ones(R_DH))


def _aprep_kernel(q_ref, k_ref, v_ref, cos_ref, sin_ref, qn_ref, kn_ref, bd_ref, qo_ref, ko_ref, vt_ref):
    tm = v_ref.shape[0]
    pad_row = lax.broadcasted_iota(jnp.int32, (V_PAD, tm), 0)
    pad = jnp.where(pad_row == 0, 1.0, 0.0).astype(BF16)
    for h in range(A_HEADS):
        vt_ref[h, 0, 0:A_DV, :] = v_ref[:, h * A_DV:(h + 1) * A_DV].astype(F32).T.astype(BF16)
        vt_ref[h, 0, A_DV:A_DV + V_PAD, :] = pad
    bd = bd_ref[...]
    cos = cos_ref[...]
    sin = sin_ref[...]
    lane = lax.broadcasted_iota(jnp.int32, (1, 512), 1)
    lower = lax.rem(lane, A_DH) < A_DH // 2

    def one(x, g):
        ms = _group_sum(x * x, bd) * (1.0 / A_DH)
        y = x * lax.rsqrt(ms + NORM_EPS) * g
        swapped = jnp.where(lower, pltpu.roll(y, 512 - A_DH // 2, 1), pltpu.roll(y, A_DH // 2, 1))
        return y * cos + swapped * sin

    qo_ref[...] = one(q_ref[...].astype(F32), qn_ref[...]).astype(BF16)
    ko_ref[...] = one(k_ref[...].astype(F32), kn_ref[...]).astype(BF16)


def _rope_tables(seq, tm):
    rows = seq // GRID_W
    row = jnp.repeat(jnp.arange(rows, dtype=F32), GRID_W)
    col = jnp.tile(jnp.arange(GRID_W, dtype=F32), rows)
    half = A_DH // 2
    inv = ROPE_BASE ** (-jnp.arange(0, half, 2, dtype=F32) / half)
    ang = jnp.concatenate([row[:, None] * inv, col[:, None] * inv], axis=-1)
    cos, sin = jnp.cos(ang), jnp.sin(ang)
    cos64 = jnp.concatenate([cos, cos], axis=-1)
    sin64 = jnp.concatenate([-sin, sin], axis=-1)
    cos_t = jnp.concatenate([jnp.tile(cos64, (1, 8)), jnp.ones((tm, 512), F32)], axis=0)
    sin_t = jnp.concatenate([jnp.tile(sin64, (1, 8)), jnp.zeros((tm, 512), F32)], axis=0)
    return cos_t, sin_t


def _aprep(z, cos_t, sin_t, qk_norm, geom):
    rows = z.shape[0]
    nl, seq, _ = geom
    tm = ROW_TILE
    perm = _deinterleave64()
    qn = jnp.tile(qk_norm[0][perm] * (A_DH ** -0.5 * LOG2_E), 8).reshape(1, 512)
    kn = jnp.tile(qk_norm[1][perm], 8).reshape(1, 512)
    tiles_per_seq = seq // tm
    tab = lambda i: (jnp.where(i < nl // tm, lax.rem(i, tiles_per_seq), tiles_per_seq), 0)
    full = lambda *s: pl.BlockSpec(s, lambda i: (0,) * len(s))
    return pl.pallas_call(
        _aprep_kernel,
        grid=(rows // tm,),
        in_specs=[pl.BlockSpec((tm, 512), lambda i: (i, Z_AQ // 512)),
                  pl.BlockSpec((tm, 512), lambda i: (i, Z_AK // 512)),
                  pl.BlockSpec((tm, 512), lambda i: (i, Z_AV // 512)),
                  pl.BlockSpec((tm, 512), tab), pl.BlockSpec((tm, 512), tab),
                  full(1, 512), full(1, 512), full(LANES, LANES)],
        out_specs=[pl.BlockSpec((tm, 512), lambda i: (i, 0)), pl.BlockSpec((tm, 512), lambda i: (i, 0)),
                   pl.BlockSpec((A_HEADS, 1, A_DV + V_PAD, tm), lambda i: (0, i, 0, 0))],
        out_shape=[jax.ShapeDtypeStruct((rows, 512), BF16), jax.ShapeDtypeStruct((rows, 512), BF16),
                   jax.ShapeDtypeStruct((A_HEADS, rows // tm, A_DV + V_PAD, tm), BF16)],
        compiler_params=pltpu.CompilerParams(dimension_semantics=("arbitrary",)),
        name='attn_prep',
    )(z, z, z, cos_t, sin_t, qn, kn, _group_ones(A_DH))


def _attn_kernel(*refs, n_lat, lam_init):
    lam_ref, sub_ref, q_ref, kc_ref, vc_ref = refs[:5]
    if n_lat:
        kl_ref, vl_ref = refs[5:7]
        o_ref, m_scr, acc_scr, *s_scr = refs[7:]
    else:
        o_ref, m_scr, acc_scr = refs[5:]

    q = q_ref[...]
    lane = lax.broadcasted_iota(jnp.int32, (1, LANES), 1)
    zero = jnp.zeros_like(q)
    qs = (jnp.where(lane < A_DH, q, zero), jnp.where(lane >= A_DH, q, zero))
    m_scr[...] = jnp.full_like(m_scr, -jnp.inf)
    acc_scr[...] = jnp.zeros_like(acc_scr)

    def scores(kb):
        return [_dot_nt(kb, qm) for qm in qs]

    def absorb(s, smax, vt):
        m_old = [m_scr[i] for i in range(2)]
        m_new = [jnp.maximum(m_old[i], smax[i]) for i in range(2)]
        p = [jnp.exp2(s[i] - m_new[i]).astype(BF16) for i in range(2)]
        pv = [_dot(vt, p[i]) for i in range(2)]
        for i in range(2):
            acc_scr[i] = jnp.exp2(m_old[i] - m_new[i]) * acc_scr[i] + pv[i]
            m_scr[i] = m_new[i]

    colmax = lambda s: [jnp.max(a, axis=0, keepdims=True) for a in s]
    s_ctx = scores(kc_ref[...])
    absorb(s_ctx, colmax(s_ctx), vc_ref[0, 0])
    if n_lat:
        def keys(c):
            return kl_ref[pl.ds(pl.multiple_of(c * KV_TILE, KV_TILE), KV_TILE), :]

        nbuf = len(s_scr) // 2
        s_buf, max_buf = s_scr[:nbuf], s_scr[nbuf:]

        def put(slot, s):
            for i in range(2):
                s_buf[slot][i] = s[i]
                max_buf[slot][i] = jnp.max(s[i], axis=0, keepdims=True)

        put(0, scores(keys(0)))

        unroll = min(ATT_UNROLL, n_lat)

        def body(j, carry):
            c = unroll * j
            for u in range(unroll):
                put((u + 1) % nbuf, scores(keys(jnp.minimum(c + u + 1, n_lat - 1))))
                slot = u % nbuf
                absorb([s_buf[slot][0], s_buf[slot][1]], [max_buf[slot][0], max_buf[slot][1]], vl_ref[0, c + u])
            return carry
        lax.fori_loop(0, n_lat // unroll, body, 0)

    lp = lam_ref[...]
    lam = (jnp.exp(jnp.sum(lp[0:1, :] * lp[1:2, :], axis=1, keepdims=True))
           - jnp.exp(jnp.sum(lp[2:3, :] * lp[3:4, :], axis=1, keepdims=True)) + lam_init)
    a0, a1 = acc_scr[0], acc_scr[1]
    o = a0[:A_DV] / a0[A_DV:A_DV + 1] - lam * (a1[:A_DV] / a1[A_DV:A_DV + 1])
    ms = jnp.mean(o * o, axis=0, keepdims=True)
    o = o * lax.rsqrt(ms + NORM_EPS) * sub_ref[...] * (1.0 - lam_init)
    o_ref[...] = o.T


def _attention(qr, kr, vt, lam_p, subln, lam_init, geom, batch, ctx_queries):
    nl, seq, ctx = geom
    ctx_blk = nl // ctx
    per_tile = KV_TILE // ctx
    full = lambda *s: pl.BlockSpec(s, lambda b, h, i: (0,) * len(s))
    kv_ctx = [pl.BlockSpec((ctx, LANES), lambda b, h, i: (ctx_blk + b, h)),
              pl.BlockSpec((1, 1, A_DV + V_PAD, ctx),
                           lambda b, h, i: (h, nl // KV_TILE + b // per_tile, 0, lax.rem(b, per_tile)))]
    head = [full(4, A_DH), full(A_DV, 1)]
    scratch = lambda tq: [pltpu.VMEM((2, 1, tq), F32), pltpu.VMEM((2, A_DV + V_PAD, tq), F32)]
    if ctx_queries:
        tq, nq, n_lat, out_rows = ctx, 1, 0, batch * ctx
        in_specs = head + [pl.BlockSpec((tq, LANES), lambda b, h, i: (ctx_blk + b, h))] + kv_ctx
        args = (lam_p, subln.reshape(A_DV, 1), qr, kr, vt)
        omap = lambda b, h, i: (b, h)
        scratch_shapes = scratch(tq)
    else:
        tq, out_rows = Q_TILE, nl
        nq, n_lat = seq // tq, seq // KV_TILE
        assert n_lat % min(ATT_UNROLL, n_lat) == 0 and min(ATT_UNROLL, n_lat) % ATT_BUFS == 0
        omap = lambda b, h, i: (b * nq + i, h)
        in_specs = head + [pl.BlockSpec((tq, LANES), omap)] + kv_ctx + [
            pl.BlockSpec((seq, LANES), lambda b, h, i: (b, h)),
            pl.BlockSpec((1, n_lat, A_DV + V_PAD, KV_TILE), lambda b, h, i: (h, b, 0, 0))]
        args = (lam_p, subln.reshape(A_DV, 1), qr, kr, vt, kr, vt)
        scratch_shapes = (scratch(tq) + [pltpu.VMEM((2, KV_TILE, tq), F32)] * ATT_BUFS
                          + [pltpu.VMEM((2, 1, tq), F32)] * ATT_BUFS)
    return pl.pallas_call(
        functools.partial(_attn_kernel, n_lat=n_lat, lam_init=lam_init),
        grid=(batch, A_HEADS, nq),
        in_specs=in_specs,
        out_specs=pl.BlockSpec((tq, LANES), omap),
        out_shape=jax.ShapeDtypeStruct((out_rows, MIX_W), F32),
        scratch_shapes=scratch_shapes,
        compiler_params=pltpu.CompilerParams(dimension_semantics=("arbitrary", "arbitrary", "arbitrary")),
        name='diff_attn_ctx' if ctx_queries else 'diff_attn',
    )(*args)


def _merge_kernel(x_ref, mod_ref, hm_ref, mo_ref, on_ref, yr_ref, ya_ref, g_ref, wb_ref, wo_ref, o_ref):
    hm = hm_ref[0] + hm_ref[1]
    on = on_ref[...]
    parts = []
    for h in range(M_HEADS):
        blk = hm[:, h * M_DV:(h + 1) * M_DV]
        ms = jnp.mean(blk * blk, axis=-1, keepdims=True)
        parts.append(blk * lax.rsqrt(ms + NORM_EPS) * on[:, h * M_DV:(h + 1) * M_DV])
    ym = jnp.concatenate(parts, axis=1) * _sigmoid(mo_ref[...].astype(F32))
    d = x_ref.shape[1]
    zsum = _sigmoid(g_ref[:, 0:d].astype(F32)) *_dot(ym.astype(BF16), wb_ref[0])
    zsum += _sigmoid(g_ref[:, d:2 * d].astype(F32)) *_dot(yr_ref[...].astype(BF16), wb_ref[1])
    zsum += _sigmoid(g_ref[:, 2 * d:3 * d].astype(F32)) *_dot(ya_ref[...].astype(BF16), wb_ref[2])
    o_ref[...] = x_ref[...] + mod_ref[0, 5:6, :] * _dot(zsum.astype(BF16), wo_ref[...])


def _merge(xr, mods, hm, z, out_norm, yr, ya, w_branch, w_o, rows, stream_of_tile):
    d = xr.shape[1]
    tm = ROW_TILE
    full = lambda *s: pl.BlockSpec(s, lambda i: (0,) * len(s))
    return pl.pallas_call(
        _merge_kernel,
        grid=(rows // tm,),
        in_specs=[pl.BlockSpec((tm, d), lambda i: (i, 0)),
                  pl.BlockSpec((1, N_ADA, d), lambda i: (stream_of_tile(i), 0, 0)),
                  pl.BlockSpec((2, tm, MIX_W), lambda i: (0, i, 0)),
                  pl.BlockSpec((tm, MIX_W), lambda i: (i, Z_MO // MIX_W)),
                  full(1, MIX_W),
                  pl.BlockSpec((tm, MIX_W), lambda i: (i, 0)),
                  pl.BlockSpec((tm, MIX_W), lambda i: (i, 0)),
                  pl.BlockSpec((tm, 3 * d), lambda i: (i, Z_G // (3 * d))),
                  full(3, MIX_W, d), full(d, d)],
        out_specs=pl.BlockSpec((tm, d), lambda i: (i, 0)),
        out_shape=jax.ShapeDtypeStruct((rows, d), F32),
        compiler_params=pltpu.CompilerParams(dimension_semantics=("arbitrary",)),
        name='merge',
    )(xr, mods, hm, z, out_norm.reshape(1, MIX_W), yr, ya, z, w_branch, w_o)


def _lambda_init(layer):
    return 0.8 - 0.6 * math.exp(-0.3 * layer)


def kernel(x, c, ctx, c_ctx, w_ada, b_ada, norm_g, ffn1_w_gu, ffn1_w_down, ffn2_w_gu, ffn2_w_down,
           w_in, m_conv, m_gate_bias, m_out_norm, r_conv, r_w0, r_w2, r_a0, r_a2, r_g2, r_kk, r_ka,
           r_rk, r_ln_w, r_ln_b, a_qk_norm, a_lambda, a_subln, w_branch, w_o):
    batch, seq, d = x.shape
    ctx_len = ctx.shape[1]
    depth = w_ada.shape[0]
    nl, nc = batch * seq, batch * ctx_len
    rows = nl + nc
    geom = (nl, seq, ctx_len)
    tm = ROW_TILE
    assert seq % tm == 0 and nc % tm == 0 and KV_TILE == tm and KV_TILE % ctx_len == 0 and nl % ctx_len == 0
    assert ctx_len % CHUNK == 0 and seq % GRID_W == 0 and d == 1024

    tiles_per_seq = seq // tm
    stream_of_tile = lambda i: jnp.minimum(i // tiles_per_seq, batch)

    cv = jnp.zeros((SUBLANES, d), F32).at[:batch].set(c).at[batch].set(c_ctx)
    mods_all = _ada_all(cv, w_ada, b_ada)[:, :batch + 1].reshape(depth, batch + 1, N_ADA, d)

    zcols = _z_column_sources()
    zsrc = jnp.asarray(np.maximum(zcols, 0))
    zmask = jnp.asarray((zcols >= 0).astype(np.float32))
    cos_t, sin_t = _rope_tables(seq, tm)

    xr = jnp.concatenate([x.reshape(nl, d), ctx.reshape(nc, d)], axis=0)
    for li in range(depth):
        last = li == depth - 1
        mods = mods_all[li]
        w_z = (jnp.take(w_in[li], zsrc, axis=1) * zmask).astype(BF16)
        w_z = w_z.reshape(d, Z_W // Z_TILE, Z_TILE).transpose(1, 0, 2)

        xr = _ffn(xr, mods, norm_g[li, 0], ffn1_w_gu[li].astype(BF16), ffn1_w_down[li].astype(BF16),
                  0, rows, stream_of_tile)
        z = _inproj(xr, mods, norm_g[li, 1], w_z, stream_of_tile)

        qk_act, gcol = _mprep(z, m_conv[li], m_gate_bias[li], geom)
        qk_hm = qk_act.reshape(rows, 2 * M_HEADS, M_DQK).transpose(1, 0, 2)
        grow = gcol[:, :16].reshape(rows // CHUNK, CHUNK, 16).transpose(0, 2, 1)

        rkv_c, g_r, vb, *scan_in = _rprep(z, r_conv[li], r_w0[li], r_w2[li], r_a0[li], r_a2[li], r_g2[li],
                                           r_kk[li], r_ka[li], geom)
        y_tm, hm = _both_scans(*scan_in[:6], vb, scan_in[6], qk_hm, z, gcol, grow, geom, batch)
        yr = _rpost(y_tm, rkv_c, g_r, r_ln_w[li], r_ln_b[li], r_rk[li])

        qr, kr, vt = _aprep(z, cos_t, sin_t, a_qk_norm[li], geom)
        att = lambda cq: _attention(qr, kr, vt, a_lambda[li], a_subln[li], _lambda_init(li), geom, batch, cq)
        ya = att(False) if last else jnp.concatenate([att(False), att(True)], axis=0)

        out_rows = nl if last else rows
        xr = _merge(xr, mods, hm, z, m_out_norm[li], yr, ya, w_branch[li].astype(BF16),
                    w_o[li].astype(BF16), out_rows, stream_of_tile)
        xr = _ffn(xr, mods, norm_g[li, 2], ffn2_w_gu[li].astype(BF16), ffn2_w_down[li].astype(BF16),
                  6, out_rows, stream_of_tile)
    return xr[:nl].reshape(batch, seq, d)
```
